```python
import math
import jax
import jax.numpy as jnp
from jax import lax
import numpy as np

D_MODEL = 2048
BATCH = 4
SEQ = 4096
DEPTH = 2

GRID_W = 64
CTX_LEN = 256
HEAD_DIM = 128
N_GROUPS = 4
HEADS_PER_GROUP = D_MODEL // (N_GROUPS * HEAD_DIM)
D_MIX = N_GROUPS * HEADS_PER_GROUP * HEAD_DIM
Q_BLOCK = 128
ROPE_THETA = 10000.0
EPS = 1e-6
NA_KH_MAX = 8
NA_KW = 16
GQA_KV_HEADS = 2
MLA_Q_RANK = 512
MLA_KV_RANK = 256
MLA_NOPE = 128
MLA_ROPE = 64
MLA_V = 128
DIFF_D = 64
D_FF = -(-8 * D_MODEL // (3 * 256)) * 256

IN_SPLITS = (
    HEADS_PER_GROUP * HEAD_DIM, HEADS_PER_GROUP * HEAD_DIM, HEADS_PER_GROUP * HEAD_DIM,
    HEADS_PER_GROUP * HEAD_DIM, GQA_KV_HEADS * HEAD_DIM, GQA_KV_HEADS * HEAD_DIM,
    MLA_Q_RANK, MLA_KV_RANK, MLA_ROPE,
    HEADS_PER_GROUP * 2 * DIFF_D, HEADS_PER_GROUP * 2 * DIFF_D, HEADS_PER_GROUP * 2 * DIFF_D,
)
D_IN = sum(IN_SPLITS)

kernel_name = 'hybrid_dit_parallel_head_groups'


def rms_norm(x, g):
    xf = x.astype(jnp.float32)
    y = xf * lax.rsqrt(jnp.mean(xf * xf, axis=-1, keepdims=True) + EPS)
    return y.astype(x.dtype) * g


def axial_rope(n_tokens, dim):
    t = jnp.arange(n_tokens, dtype=jnp.int32)
    row = (t // GRID_W).astype(jnp.float32)
    col = (t % GRID_W).astype(jnp.float32)
    d_axis = dim // 2
    inv = 1.0 / (ROPE_THETA ** (jnp.arange(0, d_axis, 2, dtype=jnp.float32) / d_axis))
    ang = jnp.concatenate([row[:, None] * inv, col[:, None] * inv], axis=-1)
    return jnp.cos(ang), jnp.sin(ang)


def apply_rope(x, cos, sin):
    x1, x2 = jnp.split(x, 2, axis=-1)
    return jnp.concatenate([x1 * cos - x2 * sin, x1 * sin + x2 * cos], axis=-1).astype(x.dtype)


def to_heads(a, n_heads):
    b, t, _ = a.shape
    return a.reshape(b, t, n_heads, -1).transpose(0, 2, 1, 3)


def over_query_blocks(fn, q):
    *lead, s, d = q.shape
    nb = s // Q_BLOCK
    qb = jnp.moveaxis(q.reshape(*lead, nb, Q_BLOCK, d), -3, 0)
    out = lax.map(fn, qb)
    out = jnp.moveaxis(out, 0, -3)
    return out.reshape(*out.shape[:-3], s, out.shape[-1])


def softmax_attend(q, k, v, scale):
    s = jnp.einsum('bkgqd,bktd->bkgqt', q, k).astype(jnp.float32) * scale
    p = jax.nn.softmax(s, axis=-1).astype(v.dtype)
    return jnp.einsum('bkgqt,bktd->bkgqd', p, v)


def neighbourhood_attention(q, k, v, k_ctx, v_ctx, rpb):
    b, h, s, d = q.shape
    rows = s // GRID_W
    kh = min(NA_KH_MAX, rows)
    scale = d ** -0.5
    qg = q.reshape(b, h, rows, GRID_W, d)
    kg = k.reshape(b, h, rows, GRID_W, d)
    vg = v.reshape(b, h, rows, GRID_W, d)
    cols = np.arange(GRID_W)
    c_start = np.clip(cols - NA_KW // 2, 0, GRID_W - NA_KW)
    col_idx = c_start[:, None] + np.arange(NA_KW)[None, :]
    dc_idx = col_idx - cols[:, None] + (NA_KW - 1)
    col_bias = rpb[:, :, dc_idx]
    n_win = kh * NA_KW

    def row_block(r):
        r_start = jnp.clip(r - kh // 2, 0, rows - kh)
        q_r = lax.dynamic_index_in_dim(qg, r, axis=2, keepdims=False)
        k_win = lax.dynamic_slice_in_dim(kg, r_start, kh, axis=2)[:, :, :, col_idx]
        v_win = lax.dynamic_slice_in_dim(vg, r_start, kh, axis=2)[:, :, :, col_idx]
        dr_idx = r_start + jnp.arange(kh) - r + (NA_KH_MAX - 1)
        bias = jnp.take(col_bias, dr_idx, axis=1).transpose(0, 2, 1, 3)
        s_win = jnp.einsum('bhcd,bhicjd->bhcij', q_r, k_win).astype(jnp.float32) * scale + bias.astype(jnp.float32)
        s_ctx = jnp.einsum('bhcd,bhtd->bhct', q_r, k_ctx).astype(jnp.float32) * scale
        p = jax.nn.softmax(jnp.concatenate([s_win.reshape(b, h, GRID_W, n_win), s_ctx], axis=-1), axis=-1).astype(v.dtype)
        p_win = p[..., :n_win].reshape(b, h, GRID_W, kh, NA_KW)
        p_ctx = p[..., n_win:]
        return jnp.einsum('bhcij,bhicjd->bhcd', p_win, v_win) + jnp.einsum('bhct,bhtd->bhcd', p_ctx, v_ctx)

    out = lax.map(row_block, jnp.arange(rows))
    return jnp.moveaxis(out, 0, 2).reshape(b, h, s, d)


def na_mixer(q_l, k_l, v_l, q_c, k_c, v_c, rpb, ctx_out):
    H = HEADS_PER_GROUP
    kc, vc = to_heads(k_c, H), to_heads(v_c, H)
    ol = neighbourhood_attention(to_heads(q_l, H), to_heads(k_l, H), to_heads(v_l, H), kc, vc, rpb)
    oc = None
    if ctx_out:
        oc = softmax_attend(to_heads(q_c, H)[:, :, None], kc, vc, HEAD_DIM ** -0.5)[:, :, 0]
    return ol, oc


def gqa_mixer(q_l, k_l, v_l, q_c, k_c, v_c, g_q, g_k, rope, ctx_out):
    H, KV = HEADS_PER_GROUP, GQA_KV_HEADS
    G = H // KV
    cos, sin = rope
    scale = HEAD_DIM ** -0.5

    def grouped(q):
        b, _, t, d = q.shape
        return q.reshape(b, KV, G, t, d)

    ql = grouped(apply_rope(rms_norm(to_heads(q_l, H), g_q), cos, sin))
    kl = apply_rope(rms_norm(to_heads(k_l, KV), g_k), cos, sin)
    kc = rms_norm(to_heads(k_c, KV), g_k)
    vc = to_heads(v_c, KV)
    k_all = jnp.concatenate([kc, kl], axis=2)
    v_all = jnp.concatenate([vc, to_heads(v_l, KV)], axis=2)
    ol = over_query_blocks(lambda qb: softmax_attend(qb, k_all, v_all, scale), ql)
    b, _, _, s, d = ol.shape
    ol = ol.reshape(b, H, s, d)
    oc = None
    if ctx_out:
        qc = grouped(rms_norm(to_heads(q_c, H), g_q))
        oc = softmax_attend(qc, kc, vc, scale).reshape(b, H, -1, d)
    return ol, oc


def mla_mixer(cq_l, ckv_l, kpe_l, cq_c, ckv_c, kpe_c, g_q, g_kv, w_uq, w_ukv, rope, ctx_out):
    H = HEADS_PER_GROUP
    cos, sin = rope
    scale = (MLA_NOPE + MLA_ROPE) ** -0.5

    def queries(cq, rot):
        q = to_heads(rms_norm(cq, g_q) @ w_uq, H)
        q_nope, q_pe = q[..., :MLA_NOPE], q[..., MLA_NOPE:]
        if rot:
            q_pe = apply_rope(q_pe, cos, sin)
        return jnp.concatenate([q_nope, q_pe], axis=-1)[:, :, None]

    def keys_values(ckv, kpe, rot):
        kv = to_heads(rms_norm(ckv, g_kv) @ w_ukv, H)
        k_nope, v = kv[..., :MLA_NOPE], kv[..., MLA_NOPE:]
        k_pe = kpe[:, None]
        if rot:
            k_pe = apply_rope(k_pe, cos, sin)
        k_pe = jnp.broadcast_to(k_pe, k_nope.shape[:-1] + (MLA_ROPE,))
        return jnp.concatenate([k_nope, k_pe], axis=-1), v

    kl, vl = keys_values(ckv_l, kpe_l, True)
    kc, vc = keys_values(ckv_c, kpe_c, False)
    k_all = jnp.concatenate([kc, kl], axis=2)
    v_all = jnp.concatenate([vc, vl], axis=2)
    ol = over_query_blocks(lambda qb: softmax_attend(qb, k_all, v_all, scale), queries(cq_l, True))[:, :, 0]
    oc = None
    if ctx_out:
        oc = softmax_attend(queries(cq_c, False), kc, vc, scale)[:, :, 0]
    return ol, oc


def diff_mixer(q_l, k_l, v_l, q_c, k_c, v_c, lq1, lk1, lq2, lk2, g_sub, lam_init, rope, ctx_out):
    H = HEADS_PER_GROUP
    cos, sin = rope
    scale = DIFF_D ** -0.5
    f32 = jnp.float32
    lam = (jnp.exp(jnp.sum(lq1.astype(f32) * lk1.astype(f32)))
           - jnp.exp(jnp.sum(lq2.astype(f32) * lk2.astype(f32))) + lam_init)

    def pair_heads(a):
        b, t, _ = a.shape
        return a.reshape(b, t, H, 2, DIFF_D).transpose(0, 2, 3, 1, 4)

    def attend(q, k, v):
        s = jnp.einsum('bhiqd,bhitd->bhiqt', q, k).astype(f32) * scale
        p = jax.nn.softmax(s, axis=-1)
        w = (p[:, :, 0] - lam * p[:, :, 1]).astype(v.dtype)
        return jnp.einsum('bhqt,bhtd->bhqd', w, v)

    def finish(o):
        return rms_norm(o, g_sub) * (1.0 - lam_init)

    ql = apply_rope(pair_heads(q_l), cos, sin)
    kl = apply_rope(pair_heads(k_l), cos, sin)
    kc = pair_heads(k_c)
    vc = to_heads(v_c, H)
    k_all = jnp.concatenate([kc, kl], axis=3)
    v_all = jnp.concatenate([vc, to_heads(v_l, H)], axis=2)
    ol = finish(over_query_blocks(lambda qb: attend(qb, k_all, v_all), ql))
    oc = None
    if ctx_out:
        oc = finish(attend(pair_heads(q_c), kc, vc))
    return ol, oc


def hybrid_mixer(h, hc, w_in, w_out, rpb, gqa_gq, gqa_gk, mla_gq, mla_gkv, mla_wuq, mla_wukv,
                 lq1, lk1, lq2, lk2, g_sub, lam_init, rope128, rope64, ctx_out):
    offsets = np.cumsum(IN_SPLITS)[:-1].tolist()
    pl = jnp.split(h @ w_in, offsets, axis=-1)
    pc = jnp.split(hc @ w_in, offsets, axis=-1)
    oa, oac = na_mixer(*pl[0:3], *pc[0:3], rpb, ctx_out)
    ob, obc = gqa_mixer(*pl[3:6], *pc[3:6], gqa_gq, gqa_gk, rope128, ctx_out)
    oc, occ = mla_mixer(*pl[6:9], *pc[6:9], mla_gq, mla_gkv, mla_wuq, mla_wukv, rope64, ctx_out)
    od, odc = diff_mixer(*pl[9:12], *pc[9:12], lq1, lk1, lq2, lk2, g_sub, lam_init, rope64, ctx_out)

    def merge(parts):
        o = jnp.concatenate(parts, axis=1)
        b, n, t, d = o.shape
        return o.transpose(0, 2, 1, 3).reshape(b, t, n * d) @ w_out

    out_c = merge([oac, obc, occ, odc]) if ctx_out else None
    return merge([oa, ob, oc, od]), out_c


def swiglu(h, wg, wu, wd):
    return (jax.nn.silu(h @ wg) * (h @ wu)) @ wd


def setup_inputs(seed: int = 0) -> dict:
    key = jax.random.key(seed)
    ks = iter(jax.random.split(key, 32))
    L, D, H = DEPTH, D_MODEL, HEADS_PER_GROUP

    def nrm(shape, scale):
        return jax.random.normal(next(ks), shape, jnp.float32) * scale

    def gain(shape):
        return 1.0 + 0.02 * jax.random.normal(next(ks), shape, jnp.float32)

    return {
        'x': nrm((BATCH, SEQ, D), 1.0),
        'c': nrm((BATCH, D), 1.0),
        'ctx': nrm((BATCH, CTX_LEN, D), 1.0),
        'c_ctx': nrm((D,), 1.0),
        'w_ada': nrm((L, D, 6 * D), D ** -0.5),
        'b_ada': nrm((L, 6 * D), 0.02),
        'g_attn': gain((L, D)),
        'g_ffn': gain((L, D)),
        'w_in': nrm((L, D, D_IN), D ** -0.5),
        'w_out': nrm((L, D_MIX, D), D_MIX ** -0.5),
        'na_rpb': nrm((L, H, 2 * NA_KH_MAX - 1, 2 * NA_KW - 1), 0.1),
        'gqa_gq': gain((L, HEAD_DIM)),
        'gqa_gk': gain((L, HEAD_DIM)),
        'mla_gq': gain((L, MLA_Q_RANK)),
        'mla_gkv': gain((L, MLA_KV_RANK)),
        'mla_wuq': nrm((L, MLA_Q_RANK, H * (MLA_NOPE + MLA_ROPE)), MLA_Q_RANK ** -0.5),
        'mla_wukv': nrm((L, MLA_KV_RANK, H * (MLA_NOPE + MLA_V)), MLA_KV_RANK ** -0.5),
        'diff_lq1': nrm((L, DIFF_D), 0.1),
        'diff_lk1': nrm((L, DIFF_D), 0.1),
        'diff_lq2': nrm((L, DIFF_D), 0.1),
        'diff_lk2': nrm((L, DIFF_D), 0.1),
        'diff_gsub': gain((L, 2 * DIFF_D)),
        'ffn_wg': nrm((L, D, D_FF), D ** -0.5),
        'ffn_wu': nrm((L, D, D_FF), D ** -0.5),
        'ffn_wd': nrm((L, D_FF, D), D_FF ** -0.5),
        'g_final': gain((D,)),
    }


def reference(x, c, ctx, c_ctx, w_ada, b_ada, g_attn, g_ffn, w_in, w_out, na_rpb, gqa_gq, gqa_gk,
              mla_gq, mla_gkv, mla_wuq, mla_wukv, diff_lq1, diff_lk1, diff_lq2, diff_lk2, diff_gsub,
              ffn_wg, ffn_wu, ffn_wd, g_final):
    s = x.shape[1]
    rope128 = axial_rope(s, HEAD_DIM)
    rope64 = axial_rope(s, MLA_ROPE)
    silu_c = jax.nn.silu(c)
    silu_cc = jax.nn.silu(c_ctx)
    xc = ctx
    for l in range(DEPTH):
        ctx_out = l < DEPTH - 1
        lam_init = 0.8 - 0.6 * math.exp(-0.3 * l)
        mod = (silu_c @ w_ada[l] + b_ada[l])[:, None, :]
        mod_c = silu_cc @ w_ada[l] + b_ada[l]
        sh_a, sc_a, gt_a, sh_f, sc_f, gt_f = jnp.split(mod, 6, axis=-1)
        csh_a, csc_a, cgt_a, csh_f, csc_f, cgt_f = jnp.split(mod_c, 6, axis=-1)
        h = rms_norm(x, g_attn[l]) * (1.0 + sc_a) + sh_a
        hc = rms_norm(xc, g_attn[l]) * (1.0 + csc_a) + csh_a
        o_l, o_c = hybrid_mixer(h, hc, w_in[l], w_out[l], na_rpb[l], gqa_gq[l], gqa_gk[l],
                                mla_gq[l], mla_gkv[l], mla_wuq[l], mla_wukv[l],
                                diff_lq1[l], diff_lk1[l], diff_lq2[l], diff_lk2[l], diff_gsub[l],
                                lam_init, rope128, rope64, ctx_out)
        x = x + gt_a * o_l
        h = rms_norm(x, g_ffn[l]) * (1.0 + sc_f) + sh_f
        x = x + gt_f * swiglu(h, ffn_wg[l], ffn_wu[l], ffn_wd[l])
        if ctx_out:
            xc = xc + cgt_a * o_c
            hc = rms_norm(xc, g_ffn[l]) * (1.0 + csc_f) + csh_f
            xc = xc + cgt_f * swiglu(hc, ffn_wg[l], ffn_wu[l], ffn_wd[l])
    return rms_norm(x, g_final)
```

```python
import functools
import math

import numpy as np
import jax
import jax.numpy as jnp
from jax import lax
from jax.experimental import pallas as pl
from jax.experimental.pallas import tpu as pltpu

F32 = jnp.float32
BF16 = jnp.bfloat16

D_MODEL = 2048
GRID_W = 64
HEAD_DIM = 128
N_HEADS = 4
ROPE_THETA = 10000.0
EPS = 1e-6
NA_KH = 8
NA_KW = 16
NA_ROWS_PER_BLOCK = 4
NA_WIN_ROWS = 12
GQA_KV_HEADS = 2
MLA_Q_RANK = 512
MLA_KV_RANK = 256
MLA_NOPE = 128
MLA_ROPE = 64
DIFF_D = 64
D_FF = 5632

NA_SCALE = HEAD_DIM ** -0.5
GQA_SCALE = HEAD_DIM ** -0.5
MLA_SCALE = (MLA_NOPE + MLA_ROPE) ** -0.5
DIFF_SCALE = DIFF_D ** -0.5
NEG_BIG = -1e30

OFF_A, OFF_B, OFF_C, OFF_D, OFF_END = 0, 1536, 2560, 3392, 4928

V7X_VMEM_LIMIT = 56 * 1024 * 1024

TQ = 256
BK = 512
TM_DENSE = 512
TF = 512


def _params(*sem):
    return pltpu.CompilerParams(dimension_semantics=sem, vmem_limit_bytes=V7X_VMEM_LIMIT)


def _dot(a, b):
    return jnp.dot(a, b, preferred_element_type=F32)


def _dot_nt(a, b):
    return lax.dot_general(a, b, (((1,), (1,)), ((), ())), preferred_element_type=F32)


def _mod_row(ref, ctx_row):
    r = pl.program_id(0) if ctx_row is None else ctx_row
    return ref[0, 0, pl.ds(r, 1), :]


def _rms(x, g):
    return x * lax.rsqrt(jnp.mean(x * x, axis=-1, keepdims=True) + EPS) * g


def _modnorm(x, g, sc, sh):
    return _rms(x, g) * (1.0 + sc) + sh


def _rope128(x, cos, sin):
    return x * cos + pltpu.roll(x, 64, 1) * sin


def _rope64(x, c, sa, sb):
    return x * c + pltpu.roll(x, 96, 1) * sa + pltpu.roll(x, 32, 1) * sb


def _mod_kernel(cc_ref, w_ref, b_ref, o_ref):
    cc = cc_ref[...]
    s = cc / (1.0 + jnp.exp(-cc))
    o_ref[0, 0] = _dot(s.astype(BF16), w_ref[0].astype(BF16)) + b_ref[0]


def _modulation(c, c_ctx, w_ada, b_ada):
    depth, d, n = w_ada.shape
    b = c.shape[0]
    assert b < 8
    cc = jnp.zeros((8, d), F32).at[:b].set(c).at[b].set(c_ctx)
    tn = 1024
    per = d // tn
    return pl.pallas_call(
        _mod_kernel,
        grid=(depth, n // tn),
        in_specs=[
            pl.BlockSpec((8, d), lambda l, j: (0, 0)),
            pl.BlockSpec((1, d, tn), lambda l, j: (l, 0, j)),
            pl.BlockSpec((1, 1, tn), lambda l, j: (l, 0, j)),
        ],
        out_specs=pl.BlockSpec((1, 1, 8, tn), lambda l, j: (l, j // per, 0, j % per)),
        out_shape=jax.ShapeDtypeStruct((depth, n // d, 8, d), F32),
        compiler_params=_params("parallel", "parallel"),
        name="adaln_mod",
    )(cc, w_ada, b_ada.reshape(depth, 1, n))


def _mod_spec(layer, chunk):
    return pl.BlockSpec((1, 1, 8, D_MODEL), lambda *_: (layer, chunk, 0, 0))


def _norm_kernel(x_ref, g_ref, sc_ref, sh_ref, o_ref, *, ctx_row):
    o_ref[0] = _modnorm(x_ref[0], g_ref[...], _mod_row(sc_ref, ctx_row),
                        _mod_row(sh_ref, ctx_row)).astype(BF16)


def _first_norm(x, g, mod, layer, ctx_row):
    b, t, d = x.shape
    tm = min(TM_DENSE, t)
    return pl.pallas_call(
        functools.partial(_norm_kernel, ctx_row=ctx_row),
        grid=(b, t // tm),
        in_specs=[
            pl.BlockSpec((1, tm, d), lambda i, j: (i, j, 0)),
            pl.BlockSpec((1, d), lambda i, j: (0, 0)),
            _mod_spec(layer, 1),
            _mod_spec(layer, 0),
        ],
        out_specs=pl.BlockSpec((1, tm, d), lambda i, j: (i, j, 0)),
        out_shape=jax.ShapeDtypeStruct((b, t, d), BF16),
        compiler_params=_params("parallel", "parallel"),
        name="first_norm",
    )(x, g.reshape(1, d), mod, mod)


def _in_a_kernel(h_ref, w_ref, o_ref):
    y = _dot(h_ref[0], w_ref[...])
    o_ref[0, :, :512] = (y[:, :512] * NA_SCALE).astype(BF16)
    o_ref[0, :, 512:] = y[:, 512:].astype(BF16)


def _in_b_kernel(h_ref, w_ref, gq_ref, gk_ref, cos_ref, sin_ref, o_ref):
    y = _dot(h_ref[0], w_ref[...])
    cos = cos_ref[...]
    sin = sin_ref[...]
    for j in range(N_HEADS + GQA_KV_HEADS):
        g = gq_ref[...] if j < N_HEADS else gk_ref[...]
        r = _rope128(_rms(y[:, 128 * j:128 * (j + 1)], g), cos, sin)
        if j < N_HEADS:
            r = r * GQA_SCALE
        o_ref[0, :, 128 * j:128 * (j + 1)] = r.astype(BF16)
    o_ref[0, :, 768:] = y[:, 768:].astype(BF16)


def _in_c_kernel(h_ref, w_ref, gq_ref, gkv_ref, wuq_ref, wukv_ref, c_ref, sa_ref, sb_ref, o_ref):
    y = _dot(h_ref[0], w_ref[...])
    c, sa, sb = c_ref[...], sa_ref[...], sb_ref[...]
    q = _dot(_rms(y[:, :512], gq_ref[...]).astype(BF16), wuq_ref[...])
    kv = _dot(_rms(y[:, 512:768], gkv_ref[...]).astype(BF16), wukv_ref[...])
    kpe = _rope64(y[:, 768:896], c, sa, sb).astype(BF16)
    for h in range(N_HEADS):
        o_ref[0, :, 256 * h:256 * h + 128] = (q[:, 256 * h:256 * h + 128] * MLA_SCALE).astype(BF16)
        qpe = _rope64(q[:, 256 * h + 128:256 * h + 256], c, sa, sb)
        o_ref[0, :, 256 * h + 128:256 * h + 256] = (qpe * MLA_SCALE).astype(BF16)
        o_ref[0, :, 1024 + 256 * h:1024 + 256 * h + 128] = kv[:, 128 * h:128 * (h + 1)].astype(BF16)
        o_ref[0, :, 1024 + 256 * h + 128:1024 + 256 * h + 256] = kpe
    o_ref[0, :, 2048:] = kv[:, 512:].astype(BF16)


def _in_d_kernel(h_ref, w_ref, c_ref, sa_ref, sb_ref, o_ref):
    y = _dot(h_ref[0], w_ref[...])
    c, sa, sb = c_ref[...], sa_ref[...], sb_ref[...]
    for j in range(2 * N_HEADS):
        r = _rope64(y[:, 128 * j:128 * (j + 1)], c, sa, sb)
        if j < N_HEADS:
            r = r * DIFF_SCALE
        o_ref[0, :, 128 * j:128 * (j + 1)] = r.astype(BF16)
    o_ref[0, :, 1024:] = y[:, 1024:].astype(BF16)


def _const_spec(shape):
    return pl.BlockSpec(shape, lambda *_: (0,) * len(shape))


def _in_proj(kernel, name, h, weights, small, tables, n_out):
    b, t, d = h.shape
    tm = min(TM_DENSE, t)
    in_specs = [pl.BlockSpec((1, tm, d), lambda i, j: (i, j, 0))]
    in_specs += [_const_spec(w.shape) for w in weights]
    in_specs += [_const_spec(s.shape) for s in small]
    in_specs += [pl.BlockSpec((tm, 128), lambda i, j: (j, 0)) for _ in tables]
    return pl.pallas_call(
        kernel,
        grid=(b, t // tm),
        in_specs=in_specs,
        out_specs=pl.BlockSpec((1, tm, n_out), lambda i, j: (i, j, 0)),
        out_shape=jax.ShapeDtypeStruct((b, t, n_out), BF16),
        compiler_params=_params("parallel", "parallel"),
        name=name,
    )(h, *weights, *small, *tables)


def _na_index_tables(rows):
    nblk = rows // NA_ROWS_PER_BLOCK
    dr = np.zeros((3, NA_ROWS_PER_BLOCK, NA_WIN_ROWS), np.int32)
    rv = np.zeros((3, NA_ROWS_PER_BLOCK, NA_WIN_ROWS), bool)
    for ty, jb in enumerate((0, 1, nblk - 1)):
        ws = NA_ROWS_PER_BLOCK * min(max(jb - 1, 0), nblk - 3)
        for a in range(NA_ROWS_PER_BLOCK):
            r = NA_ROWS_PER_BLOCK * jb + a
            r_start = min(max(r - NA_KH // 2, 0), rows - NA_KH)
            for kap in range(NA_WIN_ROWS):
                kr = ws + kap
                ok = r_start <= kr < r_start + NA_KH
                rv[ty, a, kap] = ok
                dr[ty, a, kap] = (kr - r + NA_KH - 1) if ok else 0
    cols = np.arange(GRID_W)
    c_start = np.clip(cols - NA_KW // 2, 0, GRID_W - NA_KW)
    kc = np.arange(GRID_W)[None, :]
    cv = (kc >= c_start[:, None]) & (kc < c_start[:, None] + NA_KW)
    dc = np.where(cv, kc - cols[:, None] + NA_KW - 1, 0).astype(np.int32)
    return dr, rv, dc, cv


def _na_bias(rpb, rows):
    dr, rv, dc, cv = _na_index_tables(rows)
    nh = rpb.shape[0]
    colb = rpb[:, :, dc]
    full = jnp.take(colb, jnp.asarray(dr.reshape(-1)), axis=1)
    full = full.reshape(nh, 3, NA_ROWS_PER_BLOCK, NA_WIN_ROWS, GRID_W, GRID_W)
    valid = rv[:, :, :, None, None] & cv[None, None, None, :, :]
    full = jnp.where(jnp.asarray(valid)[None], full, NEG_BIG)
    full = full.transpose(0, 1, 2, 4, 3, 5)
    return full.reshape(nh, 3, NA_ROWS_PER_BLOCK * GRID_W, NA_WIN_ROWS * GRID_W)


def _na_kernel(q_ref, k_ref, v_ref, kc_ref, vc_ref, bias_ref, o_ref, *, nblk):
    jb = pl.program_id(2)
    start = pl.multiple_of(jnp.clip(jb - 1, 0, nblk - 3) * 256, 256)
    nwin = NA_WIN_ROWS * GRID_W
    q = q_ref[0]
    kw = k_ref[0, pl.ds(start, nwin), :]
    vw = v_ref[0, pl.ds(start, nwin), :]
    s_w = _dot_nt(q, kw) + bias_ref[0, 0]
    s_c = _dot_nt(q, kc_ref[0])
    m = jnp.maximum(jnp.max(s_w, axis=-1, keepdims=True), jnp.max(s_c, axis=-1, keepdims=True))
    p_w = jnp.exp(s_w - m)
    p_c = jnp.exp(s_c - m)
    l = jnp.sum(p_w, axis=-1, keepdims=True) + jnp.sum(p_c, axis=-1, keepdims=True)
    o = _dot(p_w.astype(BF16), vw) + _dot(p_c.astype(BF16), vc_ref[0])
    o_ref[0] = (o / l).astype(BF16)


def _na_attention(pa, pa_ctx, bias):
    b, s, _ = pa.shape
    c = pa_ctx.shape[1]
    rows = s // GRID_W
    nblk = rows // NA_ROWS_PER_BLOCK
    tq = NA_ROWS_PER_BLOCK * GRID_W
    nwin = NA_WIN_ROWS * GRID_W

    def bias_map(i, h, j):
        ty = (j > 0).astype(jnp.int32) + (j == nblk - 1).astype(jnp.int32)
        return (h, ty, 0, 0)

    return pl.pallas_call(
        functools.partial(_na_kernel, nblk=nblk),
        grid=(b, N_HEADS, nblk),
        in_specs=[
            pl.BlockSpec((1, tq, 128), lambda i, h, j: (i, j, h)),
            pl.BlockSpec((1, s, 128), lambda i, h, j: (i, 0, N_HEADS + h)),
            pl.BlockSpec((1, s, 128), lambda i, h, j: (i, 0, 2 * N_HEADS + h)),
            pl.BlockSpec((1, c, 128), lambda i, h, j: (i, 0, N_HEADS + h)),
            pl.BlockSpec((1, c, 128), lambda i, h, j: (i, 0, 2 * N_HEADS + h)),
            pl.BlockSpec((1, 1, tq, nwin), bias_map),
        ],
        out_specs=pl.BlockSpec((1, tq, 128), lambda i, h, j: (i, j, h)),
        out_shape=jax.ShapeDtypeStruct((b, s, N_HEADS * 128), BF16),
        compiler_params=_params("parallel", "parallel", "arbitrary"),
        name="na_attention",
    )(pa, pa, pa, pa_ctx, pa_ctx, bias)


def _key_chunks(n_lat, n_ctx):
    chunks = []
    off = 0
    for st in range(0, n_lat, BK):
        chunks.append((0, st, min(BK, n_lat - st), off))
        off += min(BK, n_lat - st)
    for st in range(0, n_ctx, BK):
        chunks.append((1, st, min(BK, n_ctx - st), off))
        off += min(BK, n_ctx - st)
    return chunks


def _lane_fold(acc, x, op):
    for j in range(x.shape[1] // 128):
        acc = op(acc, x[:, 128 * j:128 * (j + 1)])
    return acc


def _scores_pass(q, krefs, chunks, s_ref):
    mt = jnp.full((q.shape[0], 128), -jnp.inf, F32)
    for src, st, sz, off in chunks:
        s = _dot_nt(q, krefs[src][0, st:st + sz, :])
        s_ref[:, off:off + sz] = s
        mt = _lane_fold(mt, s, jnp.maximum)
    return jnp.max(mt, axis=-1, keepdims=True)


def _attn_kernel(q_ref, *refs, n_lat, n_ctx):
    if n_lat:
        kl_ref, vl_ref, kc_ref, vc_ref, o_ref, s_ref = refs
        krefs, vrefs = (kl_ref, kc_ref), (vl_ref, vc_ref)
    else:
        kc_ref, vc_ref, o_ref, s_ref = refs
        krefs, vrefs = (None, kc_ref), (None, vc_ref)
    chunks = _key_chunks(n_lat, n_ctx)
    q = q_ref[0]
    tq = q.shape[0]
    m = _scores_pass(q, krefs, chunks, s_ref)
    lt = jnp.zeros((tq, 128), F32)
    acc = jnp.zeros((tq, 128), F32)
    for src, st, sz, off in chunks:
        p = jnp.exp(s_ref[:, off:off + sz] - m)
        lt = _lane_fold(lt, p, jnp.add)
        acc = acc + _dot(p.astype(BF16), vrefs[src][0, st:st + sz, :])
    l = jnp.sum(lt, axis=-1, keepdims=True)
    o_ref[0] = (acc / l).astype(BF16)


def _diff_kernel(q_ref, *refs, n_lat, n_ctx, lam_init):
    if n_lat:
        kl_ref, vl_ref, kc_ref, vc_ref = refs[:4]
        krefs, vrefs = (kl_ref, kc_ref), (vl_ref, vc_ref)
        rest = refs[4:]
    else:
        kc_ref, vc_ref = refs[:2]
        krefs, vrefs = (None, kc_ref), (None, vc_ref)
        rest = refs[2:]
    lq1_ref, lk1_ref, lq2_ref, lk2_ref, g_ref, o_ref, s1_ref, s2_ref = rest
    chunks = _key_chunks(n_lat, n_ctx)
    lam = (jnp.exp(jnp.sum(lq1_ref[...] * lk1_ref[...], axis=-1, keepdims=True))
           - jnp.exp(jnp.sum(lq2_ref[...] * lk2_ref[...], axis=-1, keepdims=True)) + lam_init)
    q = q_ref[0]
    tq = q.shape[0]
    lane = lax.broadcasted_iota(jnp.int32, (1, 128), 1)
    low = (lane < DIFF_D).astype(BF16)
    q1 = q * low
    q2 = q * (1.0 - low).astype(BF16)
    m1 = _scores_pass(q1, krefs, chunks, s1_ref)
    m2 = _scores_pass(q2, krefs, chunks, s2_ref)
    lt1 = jnp.zeros((tq, 128), F32)
    lt2 = jnp.zeros((tq, 128), F32)
    for _, _, sz, off in chunks:
        e1 = jnp.exp(s1_ref[:, off:off + sz] - m1)
        e2 = jnp.exp(s2_ref[:, off:off + sz] - m2)
        s1_ref[:, off:off + sz] = e1
        s2_ref[:, off:off + sz] = e2
        lt1 = _lane_fold(lt1, e1, jnp.add)
        lt2 = _lane_fold(lt2, e2, jnp.add)
    r1 = 1.0 / jnp.sum(lt1, axis=-1, keepdims=True)
    r2 = lam / jnp.sum(lt2, axis=-1, keepdims=True)
    acc = jnp.zeros((tq, 128), F32)
    for src, st, sz, off in chunks:
        w = s1_ref[:, off:off + sz] * r1 - s2_ref[:, off:off + sz] * r2
        acc = acc + _dot(w.astype(BF16), vrefs[src][0, st:st + sz, :])
    o_ref[0] = (_rms(acc, g_ref[...]) * (1.0 - lam_init)).astype(BF16)


def _attention(name, pq, p_lat, p_ctx, *, dk, q_idx, k_idx, v_idx, diff=None):
    b, t, _ = pq.shape
    n_ctx = p_ctx.shape[1]
    n_lat = 0 if p_lat is None else p_lat.shape[1]
    tq = min(TQ, t)
    in_specs = [pl.BlockSpec((1, tq, dk), lambda i, h, j: (i, j, q_idx(h)))]
    operands = [pq]
    for p, n in ((p_lat, n_lat), (p_ctx, n_ctx)):
        if p is None:
            continue
        in_specs.append(pl.BlockSpec((1, n, dk), lambda i, h, j: (i, 0, k_idx(h))))
        in_specs.append(pl.BlockSpec((1, n, 128), lambda i, h, j: (i, 0, v_idx(h))))
        operands += [p, p]
    scratch = [pltpu.VMEM((tq, n_lat + n_ctx), F32)]
    if diff is None:
        kernel = functools.partial(_attn_kernel, n_lat=n_lat, n_ctx=n_ctx)
    else:
        lam_vecs, g_sub, lam_init = diff
        kernel = functools.partial(_diff_kernel, n_lat=n_lat, n_ctx=n_ctx, lam_init=lam_init)
        in_specs += [_const_spec((1, DIFF_D))] * 4 + [_const_spec((1, 128))]
        operands += [v.reshape(1, DIFF_D) for v in lam_vecs] + [g_sub.reshape(1, 128)]
        scratch.append(pltpu.VMEM((tq, n_lat + n_ctx), F32))
    return pl.pallas_call(
        kernel,
        grid=(b, N_HEADS, t // tq),
        in_specs=in_specs,
        out_specs=pl.BlockSpec((1, tq, 128), lambda i, h, j: (i, j, h)),
        out_shape=jax.ShapeDtypeStruct((b, t, N_HEADS * 128), BF16),
        scratch_shapes=scratch,
        compiler_params=_params("parallel", "parallel", "arbitrary"),
        name=name,
    )(*operands)


def _out_kernel(oa_ref, ob_ref, oc_ref, od_ref, w_ref, x_ref, gt_ref, g_ref, sc_ref, sh_ref,
                xo_ref, ho_ref, *, ctx_row):
    y = _dot(oa_ref[0], w_ref[0:512, :])
    y = y + _dot(ob_ref[0], w_ref[512:1024, :])
    y = y + _dot(oc_ref[0], w_ref[1024:1536, :])
    y = y + _dot(od_ref[0], w_ref[1536:2048, :])
    xn = x_ref[0] + _mod_row(gt_ref, ctx_row) * y
    xo_ref[0] = xn
    ho_ref[0] = _modnorm(xn, g_ref[...], _mod_row(sc_ref, ctx_row),
                         _mod_row(sh_ref, ctx_row)).astype(BF16)


def _out_proj(outs, w_out, x, g_ffn, mod, layer, ctx_row):
    b, t, d = x.shape
    tm = min(TM_DENSE, t)
    o_spec = pl.BlockSpec((1, tm, 512), lambda i, j: (i, j, 0))
    x_spec = pl.BlockSpec((1, tm, d), lambda i, j: (i, j, 0))
    return pl.pallas_call(
        functools.partial(_out_kernel, ctx_row=ctx_row),
        grid=(b, t // tm),
        in_specs=[o_spec] * 4 + [
            _const_spec(w_out.shape),
            x_spec,
            _mod_spec(layer, 2),
            _const_spec((1, d)),
            _mod_spec(layer, 4),
            _mod_spec(layer, 3),
        ],
        out_specs=[x_spec, x_spec],
        out_shape=[jax.ShapeDtypeStruct((b, t, d), F32), jax.ShapeDtypeStruct((b, t, d), BF16)],
        compiler_params=_params("parallel", "parallel"),
        name="out_proj",
    )(*outs, w_out, x, mod, g_ffn.reshape(1, d), mod, mod)


def _ffn_kernel(h_ref, wg_ref, wu_ref, wd_ref, x_ref, gt_ref, g_ref, *rest, ctx_row, final):
    if final:
        o_ref, acc_ref = rest
    else:
        sc_ref, sh_ref, xo_ref, ho_ref, acc_ref = rest
    f = pl.program_id(2)
    h = h_ref[0]
    gate = _dot(h, wg_ref[...])
    up = _dot(h, wu_ref[...])
    act = (gate / (1.0 + jnp.exp(-gate))) * up
    part = _dot(act.astype(BF16), wd_ref[...])

    @pl.when(f == 0)
    def _():
        acc_ref[...] = part

    @pl.when(f > 0)
    def _():
        acc_ref[...] += part

    @pl.when(f == pl.num_programs(2) - 1)
    def _():
        xn = x_ref[0] + _mod_row(gt_ref, ctx_row) * acc_ref[...]
        if final:
            o_ref[0] = _rms(xn, g_ref[...])
        else:
            xo_ref[0] = xn
            ho_ref[0] = _modnorm(xn, g_ref[...], _mod_row(sc_ref, ctx_row),
                                 _mod_row(sh_ref, ctx_row)).astype(BF16)


def _ffn(h, wg, wu, wd, x, g_next, mod, layer, ctx_row, final):
    b, t, d = x.shape
    dff = wg.shape[1]
    tm = min(TM_DENSE, t)
    row_spec = pl.BlockSpec((1, tm, d), lambda i, j, f: (i, j, 0))
    in_specs = [
        row_spec,
        pl.BlockSpec((d, TF), lambda i, j, f: (0, f)),
        pl.BlockSpec((d, TF), lambda i, j, f: (0, f)),
        pl.BlockSpec((TF, d), lambda i, j, f: (f, 0)),
        row_spec,
        _mod_spec(layer, 5),
        _const_spec((1, d)),
    ]
    operands = [h, wg, wu, wd, x, mod, g_next.reshape(1, d)]
    if final:
        out_specs = row_spec
        out_shape = jax.ShapeDtypeStruct((b, t, d), F32)
    else:
        in_specs += [_mod_spec(layer + 1, 1), _mod_spec(layer + 1, 0)]
        operands += [mod, mod]
        out_specs = [row_spec, row_spec]
        out_shape = [jax.ShapeDtypeStruct((b, t, d), F32), jax.ShapeDtypeStruct((b, t, d), BF16)]
    return pl.pallas_call(
        functools.partial(_ffn_kernel, ctx_row=ctx_row, final=final),
        grid=(b, t // tm, dff // TF),
        in_specs=in_specs,
        out_specs=out_specs,
        out_shape=out_shape,
        scratch_shapes=[pltpu.VMEM((tm, d), F32)],
        compiler_params=_params("parallel", "parallel", "arbitrary"),
        name="ffn",
    )(*operands)


def _rope_tables(s, n_ctx):
    t = jnp.arange(s, dtype=jnp.int32)
    row = (t // GRID_W).astype(F32)
    col = (t % GRID_W).astype(F32)

    def cos_sin(dim):
        d_axis = dim // 2
        inv = 1.0 / (ROPE_THETA ** (jnp.arange(0, d_axis, 2, dtype=F32) / d_axis))
        ang = jnp.concatenate([row[:, None] * inv, col[:, None] * inv], axis=-1)
        return jnp.cos(ang), jnp.sin(ang)

    c128, s128 = cos_sin(HEAD_DIM)
    c64, s64 = cos_sin(MLA_ROPE)
    z = jnp.zeros_like(s64)
    lat = (
        jnp.concatenate([c128, c128], axis=-1),
        jnp.concatenate([-s128, s128], axis=-1),
        jnp.tile(c64, (1, 4)),
        jnp.tile(jnp.concatenate([-s64, z], axis=-1), (1, 2)),
        jnp.tile(jnp.concatenate([z, s64], axis=-1), (1, 2)),
    )
    one = jnp.ones((n_ctx, 128), F32)
    zero = jnp.zeros((n_ctx, 128), F32)
    return lat, (one, zero, one, zero, zero)


def kernel(x, c, ctx, c_ctx, w_ada, b_ada, g_attn, g_ffn, w_in, w_out, na_rpb, gqa_gq, gqa_gk,
           mla_gq, mla_gkv, mla_wuq, mla_wukv, diff_lq1, diff_lk1, diff_lq2, diff_lk2, diff_gsub,
           ffn_wg, ffn_wu, ffn_wd, g_final):
    depth = w_ada.shape[0]
    b, s, d = x.shape
    n_ctx = ctx.shape[1]
    rows = s // GRID_W
    ctx_row = b

    mod = _modulation(c, c_ctx, w_ada, b_ada)
    tabs_lat, tabs_ctx = _rope_tables(s, n_ctx)

    xl, xc = x, ctx
    hl = _first_norm(xl, g_attn[0], mod, 0, None)
    hc = _first_norm(xc, g_attn[0], mod, 0, ctx_row)

    for l in range(depth):
        ctx_out = l < depth - 1
        lam_init = 0.8 - 0.6 * math.exp(-0.3 * l)
        wl = w_in[l].astype(BF16)
        w_a = wl[:, OFF_A:OFF_B]
        w_b = wl[:, OFF_B:OFF_C]
        w_c = jnp.pad(wl[:, OFF_C:OFF_D], ((0, 0), (0, 128 - MLA_ROPE)))
        w_d = wl[:, OFF_D:OFF_END]
        wuq = jnp.pad(mla_wuq[l].astype(BF16).reshape(MLA_Q_RANK, N_HEADS, MLA_NOPE + MLA_ROPE),
                      ((0, 0), (0, 0), (0, 256 - MLA_NOPE - MLA_ROPE))).reshape(MLA_Q_RANK, N_HEADS * 256)
        wukv = mla_wukv[l].astype(BF16).reshape(MLA_KV_RANK, N_HEADS, 2, 128)
        wukv = wukv.transpose(0, 2, 1, 3).reshape(MLA_KV_RANK, 2 * N_HEADS * 128)
        wo = w_out[l].astype(BF16)
        wg, wu, wd = ffn_wg[l].astype(BF16), ffn_wu[l].astype(BF16), ffn_wd[l].astype(BF16)
        gq, gk = gqa_gq[l].reshape(1, 128), gqa_gk[l].reshape(1, 128)
        mgq, mgkv = mla_gq[l].reshape(1, MLA_Q_RANK), mla_gkv[l].reshape(1, MLA_KV_RANK)
        bias = _na_bias(na_rpb[l], rows)
        diff = ((diff_lq1[l], diff_lk1[l], diff_lq2[l], diff_lk2[l]), diff_gsub[l], lam_init)

        def project(h, tabs):
            t128, t64 = tabs[:2], tabs[2:]
            pa = _in_proj(_in_a_kernel, "in_proj_na", h, [w_a], [], [], 1536)
            pb = _in_proj(_in_b_kernel, "in_proj_gqa", h, [w_b], [gq, gk], t128, 1024)
            pc = _in_proj(_in_c_kernel, "in_proj_mla", h, [w_c], [mgq, mgkv, wuq, wukv], t64, 2560)
            pd = _in_proj(_in_d_kernel, "in_proj_diff", h, [w_d], [], t64, 1536)
            return pa, pb, pc, pd

        pa, pb, pc, pd = project(hl, tabs_lat)
        pa_c, pb_c, pc_c, pd_c = project(hc, tabs_ctx)

        gqa_idx = dict(dk=128, q_idx=lambda h: h, k_idx=lambda h: 4 + h // 2, v_idx=lambda h: 6 + h // 2)
        mla_idx = dict(dk=256, q_idx=lambda h: h, k_idx=lambda h: 4 + h, v_idx=lambda h: 16 + h)
        std_idx = dict(dk=128, q_idx=lambda h: h, k_idx=lambda h: 4 + h, v_idx=lambda h: 8 + h)

        outs = (
            _na_attention(pa, pa_c, bias),
            _attention("gqa_attention", pb, pb, pb_c, **gqa_idx),
            _attention("mla_attention", pc, pc, pc_c, **mla_idx),
            _attention("diff_attention", pd, pd, pd_c, diff=diff, **std_idx),
        )
        xl, hl = _out_proj(outs, wo, xl, g_ffn[l], mod, l, None)
        if ctx_out:
            outs_c = (
                _attention("na_ctx_attention", pa_c, None, pa_c, **std_idx),
                _attention("gqa_ctx_attention", pb_c, None, pb_c, **gqa_idx),
                _attention("mla_ctx_attention", pc_c, None, pc_c, **mla_idx),
                _attention("diff_ctx_attention", pd_c, None, pd_c, diff=diff, **std_idx),
            )
            xc, hc = _out_proj(outs_c, wo, xc, g_ffn[l], mod, l, ctx_row)
            xl, hl = _ffn(hl, wg, wu, wd, xl, g_attn[l + 1], mod, l, None, False)
            xc, hc = _ffn(hc, wg, wu, wd, xc, g_attn[l + 1], mod, l, ctx_row, False)
        else:
            xl = _ffn(hl, wg, wu, wd, xl, g_final, mod, l, None, True)
    return xl
```

```python
import functools
import math

import numpy as np
import jax
import jax.numpy as jnp
from jax import lax
from jax.experimental import pallas as pl
from jax.experimental.pallas import tpu as pltpu

F32 = jnp.float32
BF16 = jnp.bfloat16

D_MODEL = 2048
GRID_W = 64
HEAD_DIM = 128
N_HEADS = 4
ROPE_THETA = 10000.0
EPS = 1e-6
NA_KH = 8
NA_KW = 16
NA_ROWS_PER_BLOCK = 4
NA_WIN_ROWS = 12
GQA_KV_HEADS = 2
MLA_Q_RANK = 512
MLA_KV_RANK = 256
MLA_NOPE = 128
MLA_ROPE = 64
DIFF_D = 64
D_FF = 5632

NA_SCALE = HEAD_DIM ** -0.5
GQA_SCALE = HEAD_DIM ** -0.5
MLA_SCALE = (MLA_NOPE + MLA_ROPE) ** -0.5
DIFF_SCALE = DIFF_D ** -0.5
NEG_BIG = -1e30

OFF_A, OFF_B, OFF_C, OFF_D, OFF_END = 0, 1536, 2560, 3392, 4928

V7X_VMEM_LIMIT = 56 * 1024 * 1024

TQ = 512
BK = 512
TM_DENSE = 512
TM_UP = 1024
TM_DOWN = 256
TF = 512
ROW_SUB = 256


def _row_subtiles(tm):
    rs = min(ROW_SUB, tm)
    return [slice(r, r + rs) for r in range(0, tm, rs)]


def _params(*sem):
    return pltpu.CompilerParams(dimension_semantics=sem, vmem_limit_bytes=V7X_VMEM_LIMIT)


def _dot(a, b):
    return jnp.dot(a, b, preferred_element_type=F32)


def _dot_nt(a, b):
    return lax.dot_general(a, b, (((1,), (1,)), ((), ())), preferred_element_type=F32)


def _mod_row(ref, ctx_row):
    r = pl.program_id(0) if ctx_row is None else ctx_row
    return ref[0, 0, pl.ds(r, 1), :]


def _rms(x, g):
    return x * lax.rsqrt(jnp.mean(x * x, axis=-1, keepdims=True) + EPS) * g


def _modnorm(x, g, sc, sh):
    return _rms(x, g) * (1.0 + sc) + sh


def _rope128(x, cos, sin):
    return x * cos + pltpu.roll(x, 64, 1) * sin


def _rope64(x, c, sa, sb):
    return x * c + pltpu.roll(x, 96, 1) * sa + pltpu.roll(x, 32, 1) * sb


def _mod_kernel(cc_ref, w_ref, b_ref, o_ref):
    cc = cc_ref[...]
    s = cc / (1.0 + jnp.exp(-cc))
    o_ref[0, 0] = _dot(s.astype(BF16), w_ref[0].astype(BF16)) + b_ref[0]


def _modulation(c, c_ctx, w_ada, b_ada):
    depth, d, n = w_ada.shape
    b = c.shape[0]
    assert b < 8
    cc = jnp.zeros((8, d), F32).at[:b].set(c).at[b].set(c_ctx)
    tn = 1024
    per = d // tn
    return pl.pallas_call(
        _mod_kernel,
        grid=(depth, n // tn),
        in_specs=[
            pl.BlockSpec((8, d), lambda l, j: (0, 0)),
            pl.BlockSpec((1, d, tn), lambda l, j: (l, 0, j)),
            pl.BlockSpec((1, 1, tn), lambda l, j: (l, 0, j)),
        ],
        out_specs=pl.BlockSpec((1, 1, 8, tn), lambda l, j: (l, j // per, 0, j % per)),
        out_shape=jax.ShapeDtypeStruct((depth, n // d, 8, d), F32),
        compiler_params=_params("parallel", "parallel"),
        name="adaln_mod",
    )(cc, w_ada, b_ada.reshape(depth, 1, n))


def _mod_spec(layer, chunk):
    return pl.BlockSpec((1, 1, 8, D_MODEL), lambda *_: (layer, chunk, 0, 0))


def _norm_kernel(x_ref, g_ref, sc_ref, sh_ref, o_ref, *, ctx_row):
    o_ref[0] = _modnorm(x_ref[0], g_ref[...], _mod_row(sc_ref, ctx_row),
                        _mod_row(sh_ref, ctx_row)).astype(BF16)


def _first_norm(x, g, mod, layer, ctx_row):
    b, t, d = x.shape
    tm = min(TM_DENSE, t)
    return pl.pallas_call(
        functools.partial(_norm_kernel, ctx_row=ctx_row),
        grid=(b, t // tm),
        in_specs=[
            pl.BlockSpec((1, tm, d), lambda i, j: (i, j, 0)),
            pl.BlockSpec((1, d), lambda i, j: (0, 0)),
            _mod_spec(layer, 1),
            _mod_spec(layer, 0),
        ],
        out_specs=pl.BlockSpec((1, tm, d), lambda i, j: (i, j, 0)),
        out_shape=jax.ShapeDtypeStruct((b, t, d), BF16),
        compiler_params=_params("parallel", "parallel"),
        name="first_norm",
    )(x, g.reshape(1, d), mod, mod)


def _in_a_kernel(h_ref, w_ref, o_ref):
    for rows in _row_subtiles(h_ref.shape[1]):
        y = _dot(h_ref[0, rows, :], w_ref[...])
        o_ref[0, rows, :512] = (y[:, :512] * NA_SCALE).astype(BF16)
        o_ref[0, rows, 512:] = y[:, 512:].astype(BF16)


def _in_b_kernel(h_ref, w_ref, gq_ref, gk_ref, cos_ref, sin_ref, o_ref):
    for rows in _row_subtiles(h_ref.shape[1]):
        y = _dot(h_ref[0, rows, :], w_ref[...])
        cos = cos_ref[rows, :]
        sin = sin_ref[rows, :]
        for j in range(N_HEADS + GQA_KV_HEADS):
            g = gq_ref[...] if j < N_HEADS else gk_ref[...]
            r = _rope128(_rms(y[:, 128 * j:128 * (j + 1)], g), cos, sin)
            if j < N_HEADS:
                r = r * GQA_SCALE
            o_ref[0, rows, 128 * j:128 * (j + 1)] = r.astype(BF16)
        o_ref[0, rows, 768:] = y[:, 768:].astype(BF16)


def _in_c_kernel(h_ref, w_ref, gq_ref, gkv_ref, wuq_ref, wukv_ref, c_ref, sa_ref, sb_ref, o_ref):
    for rows in _row_subtiles(h_ref.shape[1]):
        y = _dot(h_ref[0, rows, :], w_ref[...])
        c, sa, sb = c_ref[rows, :], sa_ref[rows, :], sb_ref[rows, :]
        q = _dot(_rms(y[:, :512], gq_ref[...]).astype(BF16), wuq_ref[...])
        kv = _dot(_rms(y[:, 512:768], gkv_ref[...]).astype(BF16), wukv_ref[...])
        kpe = _rope64(y[:, 768:896], c, sa, sb).astype(BF16)
        for h in range(N_HEADS):
            o_ref[0, rows, 256 * h:256 * h + 128] = (q[:, 256 * h:256 * h + 128] * MLA_SCALE).astype(BF16)
            qpe = _rope64(q[:, 256 * h + 128:256 * h + 256], c, sa, sb)
            o_ref[0, rows, 256 * h + 128:256 * h + 256] = (qpe * MLA_SCALE).astype(BF16)
            o_ref[0, rows, 1024 + 256 * h:1024 + 256 * h + 128] = kv[:, 128 * h:128 * (h + 1)].astype(BF16)
            o_ref[0, rows, 1024 + 256 * h + 128:1024 + 256 * h + 256] = kpe
        o_ref[0, rows, 2048:] = kv[:, 512:].astype(BF16)


def _in_d_kernel(h_ref, w_ref, c_ref, sa_ref, sb_ref, o_ref):
    for rows in _row_subtiles(h_ref.shape[1]):
        y = _dot(h_ref[0, rows, :], w_ref[...])
        c, sa, sb = c_ref[rows, :], sa_ref[rows, :], sb_ref[rows, :]
        for j in range(2 * N_HEADS):
            r = _rope64(y[:, 128 * j:128 * (j + 1)], c, sa, sb)
            if j < N_HEADS:
                r = r * DIFF_SCALE
            o_ref[0, rows, 128 * j:128 * (j + 1)] = r.astype(BF16)
        o_ref[0, rows, 1024:] = y[:, 1024:].astype(BF16)


def _const_spec(shape):
    return pl.BlockSpec(shape, lambda *_: (0,) * len(shape))


def _in_proj(kernel, name, h, weights, small, tables, n_out):
    b, t, d = h.shape
    tm = min(TM_DENSE, t)
    in_specs = [pl.BlockSpec((1, tm, d), lambda i, j: (i, j, 0))]
    in_specs += [_const_spec(w.shape) for w in weights]
    in_specs += [_const_spec(s.shape) for s in small]
    in_specs += [pl.BlockSpec((tm, 128), lambda i, j: (j, 0)) for _ in tables]
    return pl.pallas_call(
        kernel,
        grid=(b, t // tm),
        in_specs=in_specs,
        out_specs=pl.BlockSpec((1, tm, n_out), lambda i, j: (i, j, 0)),
        out_shape=jax.ShapeDtypeStruct((b, t, n_out), BF16),
        compiler_params=_params("parallel", "parallel"),
        name=name,
    )(h, *weights, *small, *tables)


def _na_index_tables(rows):
    nblk = rows // NA_ROWS_PER_BLOCK
    dr = np.zeros((3, NA_ROWS_PER_BLOCK, NA_WIN_ROWS), np.int32)
    rv = np.zeros((3, NA_ROWS_PER_BLOCK, NA_WIN_ROWS), bool)
    for ty, jb in enumerate((0, 1, nblk - 1)):
        ws = NA_ROWS_PER_BLOCK * min(max(jb - 1, 0), nblk - 3)
        for a in range(NA_ROWS_PER_BLOCK):
            r = NA_ROWS_PER_BLOCK * jb + a
            r_start = min(max(r - NA_KH // 2, 0), rows - NA_KH)
            for kap in range(NA_WIN_ROWS):
                kr = ws + kap
                ok = r_start <= kr < r_start + NA_KH
                rv[ty, a, kap] = ok
                dr[ty, a, kap] = (kr - r + NA_KH - 1) if ok else 0
    cols = np.arange(GRID_W)
    c_start = np.clip(cols - NA_KW // 2, 0, GRID_W - NA_KW)
    kc = np.arange(GRID_W)[None, :]
    cv = (kc >= c_start[:, None]) & (kc < c_start[:, None] + NA_KW)
    dc = np.where(cv, kc - cols[:, None] + NA_KW - 1, 0).astype(np.int32)
    return dr, rv, dc, cv


def _na_bias(rpb, rows):
    dr, rv, dc, cv = _na_index_tables(rows)
    nh = rpb.shape[0]
    colb = rpb[:, :, dc]
    full = jnp.take(colb, jnp.asarray(dr.reshape(-1)), axis=1)
    full = full.reshape(nh, 3, NA_ROWS_PER_BLOCK, NA_WIN_ROWS, GRID_W, GRID_W)
    valid = rv[:, :, :, None, None] & cv[None, None, None, :, :]
    full = jnp.where(jnp.asarray(valid)[None], full, NEG_BIG)
    full = full.transpose(0, 1, 2, 4, 3, 5)
    return full.reshape(nh, 3, NA_ROWS_PER_BLOCK * GRID_W, NA_WIN_ROWS * GRID_W)


def _lane_fold(x, op):
    acc = x[:, :128]
    for j in range(1, x.shape[1] // 128):
        acc = op(acc, x[:, 128 * j:128 * (j + 1)])
    return acc


def _online_update(state, s, v):
    mc = jnp.max(_lane_fold(s, jnp.maximum), axis=-1, keepdims=True)
    if state is None:
        p = jnp.exp(s - mc)
        return mc, _lane_fold(p, jnp.add), _dot(p.astype(BF16), v)
    m, lt, acc = state
    m_new = jnp.maximum(m, mc)
    alpha = jnp.exp(m - m_new)
    p = jnp.exp(s - m_new)
    return m_new, lt * alpha + _lane_fold(p, jnp.add), acc * alpha + _dot(p.astype(BF16), v)


def _softmax_finish(state):
    _, lt, acc = state
    return acc / jnp.sum(lt, axis=-1, keepdims=True)


def _na_kernel(q_ref, k_ref, v_ref, kc_ref, vc_ref, bias_ref, o_ref, *, nblk):
    jb = pl.program_id(1)
    start = pl.multiple_of(jnp.clip(jb - 1, 0, nblk - 3) * 256, 256)
    nwin = NA_WIN_ROWS * GRID_W
    for h in range(N_HEADS):
        cols = slice(128 * h, 128 * (h + 1))
        q = q_ref[0, :, cols]
        s_w = _dot_nt(q, k_ref[0, pl.ds(start, nwin), cols]) + bias_ref[h, 0]
        state = _online_update(None, s_w, v_ref[0, pl.ds(start, nwin), cols])
        state = _online_update(state, _dot_nt(q, kc_ref[0, :, cols]), vc_ref[0, :, cols])
        o_ref[0, :, cols] = _softmax_finish(state).astype(BF16)


def _na_attention(pa, pa_ctx, bias):
    b, s, _ = pa.shape
    c = pa_ctx.shape[1]
    rows = s // GRID_W
    nblk = rows // NA_ROWS_PER_BLOCK
    tq = NA_ROWS_PER_BLOCK * GRID_W
    nwin = NA_WIN_ROWS * GRID_W
    hw = N_HEADS * 128

    def bias_map(i, j):
        ty = (j > 0).astype(jnp.int32) + (j == nblk - 1).astype(jnp.int32)
        return (0, ty, 0, 0)

    return pl.pallas_call(
        functools.partial(_na_kernel, nblk=nblk),
        grid=(b, nblk),
        in_specs=[
            pl.BlockSpec((1, tq, hw), lambda i, j: (i, j, 0)),
            pl.BlockSpec((1, s, hw), lambda i, j: (i, 0, 1)),
            pl.BlockSpec((1, s, hw), lambda i, j: (i, 0, 2)),
            pl.BlockSpec((1, c, hw), lambda i, j: (i, 0, 1)),
            pl.BlockSpec((1, c, hw), lambda i, j: (i, 0, 2)),
            pl.BlockSpec((N_HEADS, 1, tq, nwin), bias_map),
        ],
        out_specs=pl.BlockSpec((1, tq, hw), lambda i, j: (i, j, 0)),
        out_shape=jax.ShapeDtypeStruct((b, s, hw), BF16),
        compiler_params=_params("parallel", "arbitrary"),
        name="na_attention",
    )(pa, pa, pa, pa_ctx, pa_ctx, bias)


def _key_chunks(krefs, vrefs):
    chunks = []
    for k_ref, v_ref in zip(krefs, vrefs):
        n = k_ref.shape[1]
        chunks += [(k_ref, v_ref, st, min(BK, n - st)) for st in range(0, n, BK)]
    return chunks


def _attn_kernel(q_ref, *refs):
    o_ref = refs[-1]
    chunks = _key_chunks(refs[0:-1:2], refs[1:-1:2])
    q = q_ref[0]
    state = None
    for k_ref, v_ref, st, sz in chunks:
        state = _online_update(state, _dot_nt(q, k_ref[0, st:st + sz, :]), v_ref[0, st:st + sz, :])
    o_ref[0] = _softmax_finish(state).astype(BF16)


def _diff_kernel(q_ref, *refs, lam_init):
    lq1_ref, lk1_ref, lq2_ref, lk2_ref, g_ref, o_ref = refs[-6:]
    chunks = _key_chunks(refs[0:-6:2], refs[1:-6:2])
    lam = (jnp.exp(jnp.sum(lq1_ref[...] * lk1_ref[...], axis=-1, keepdims=True))
           - jnp.exp(jnp.sum(lq2_ref[...] * lk2_ref[...], axis=-1, keepdims=True)) + lam_init)
    q = q_ref[0]
    lane = lax.broadcasted_iota(jnp.int32, (1, 128), 1)
    low = (lane < DIFF_D).astype(BF16)
    q1 = q * low
    q2 = q * (1.0 - low).astype(BF16)
    st1 = st2 = None
    for k_ref, v_ref, st, sz in chunks:
        k = k_ref[0, st:st + sz, :]
        v = v_ref[0, st:st + sz, :]
        st1 = _online_update(st1, _dot_nt(q1, k), v)
        st2 = _online_update(st2, _dot_nt(q2, k), v)
    o = _softmax_finish(st1) - lam * _softmax_finish(st2)
    o_ref[0] = (_rms(o, g_ref[...]) * (1.0 - lam_init)).astype(BF16)


def _attention(name, pq, p_lat, p_ctx, *, dk, q_idx, k_idx, v_idx, diff=None):
    b, t, _ = pq.shape
    tq = min(TQ, t)
    in_specs = [pl.BlockSpec((1, tq, dk), lambda i, h, j: (i, j, q_idx(h)))]
    operands = [pq]
    for p in (p_lat, p_ctx):
        if p is None:
            continue
        n = p.shape[1]
        in_specs.append(pl.BlockSpec((1, n, dk), lambda i, h, j: (i, 0, k_idx(h))))
        in_specs.append(pl.BlockSpec((1, n, 128), lambda i, h, j: (i, 0, v_idx(h))))
        operands += [p, p]
    if diff is None:
        kernel = _attn_kernel
    else:
        lam_vecs, g_sub, lam_init = diff
        kernel = functools.partial(_diff_kernel, lam_init=lam_init)
        in_specs += [_const_spec((1, DIFF_D))] * 4 + [_const_spec((1, 128))]
        operands += [v.reshape(1, DIFF_D) for v in lam_vecs] + [g_sub.reshape(1, 128)]
    return pl.pallas_call(
        kernel,
        grid=(b, N_HEADS, t // tq),
        in_specs=in_specs,
        out_specs=pl.BlockSpec((1, tq, 128), lambda i, h, j: (i, j, h)),
        out_shape=jax.ShapeDtypeStruct((b, t, N_HEADS * 128), BF16),
        compiler_params=_params("parallel", "parallel", "arbitrary"),
        name=name,
    )(*operands)


def _out_kernel(oa_ref, ob_ref, oc_ref, od_ref, w_ref, x_ref, gt_ref, g_ref, sc_ref, sh_ref,
                xo_ref, ho_ref, *, ctx_row):
    gate, sc, sh = _mod_row(gt_ref, ctx_row), _mod_row(sc_ref, ctx_row), _mod_row(sh_ref, ctx_row)
    for rows in _row_subtiles(x_ref.shape[1]):
        y = _dot(oa_ref[0, rows, :], w_ref[0:512, :])
        y = y + _dot(ob_ref[0, rows, :], w_ref[512:1024, :])
        y = y + _dot(oc_ref[0, rows, :], w_ref[1024:1536, :])
        y = y + _dot(od_ref[0, rows, :], w_ref[1536:2048, :])
        xn = x_ref[0, rows, :] + gate * y
        xo_ref[0, rows, :] = xn
        ho_ref[0, rows, :] = _modnorm(xn, g_ref[...], sc, sh).astype(BF16)


def _out_proj(outs, w_out, x, g_ffn, mod, layer, ctx_row):
    b, t, d = x.shape
    tm = min(TM_DENSE, t)
    o_spec = pl.BlockSpec((1, tm, 512), lambda i, j: (i, j, 0))
    x_spec = pl.BlockSpec((1, tm, d), lambda i, j: (i, j, 0))
    return pl.pallas_call(
        functools.partial(_out_kernel, ctx_row=ctx_row),
        grid=(b, t // tm),
        in_specs=[o_spec] * 4 + [
            _const_spec(w_out.shape),
            x_spec,
            _mod_spec(layer, 2),
            _const_spec((1, d)),
            _mod_spec(layer, 4),
            _mod_spec(layer, 3),
        ],
        out_specs=[x_spec, x_spec],
        out_shape=[jax.ShapeDtypeStruct((b, t, d), F32), jax.ShapeDtypeStruct((b, t, d), BF16)],
        compiler_params=_params("parallel", "parallel"),
        name="out_proj",
    )(*outs, w_out, x, mod, g_ffn.reshape(1, d), mod, mod)


def _ffn_up_kernel(h_ref, wg_ref, wu_ref, o_ref):
    for rows in _row_subtiles(h_ref.shape[1]):
        h = h_ref[0, rows, :]
        gate = _dot(h, wg_ref[...])
        up = _dot(h, wu_ref[...])
        o_ref[0, rows, :] = ((gate / (1.0 + jnp.exp(-gate))) * up).astype(BF16)


def _ffn_up(h, wg, wu):
    b, t, d = h.shape
    dff = wg.shape[1]
    tm = min(TM_UP, t)
    w_spec = pl.BlockSpec((d, TF), lambda f, i, j: (0, f))
    return pl.pallas_call(
        _ffn_up_kernel,
        grid=(dff // TF, b, t // tm),
        in_specs=[pl.BlockSpec((1, tm, d), lambda f, i, j: (i, j, 0)), w_spec, w_spec],
        out_specs=pl.BlockSpec((1, tm, TF), lambda f, i, j: (i, j, f)),
        out_shape=jax.ShapeDtypeStruct((b, t, dff), BF16),
        compiler_params=_params("parallel", "parallel", "parallel"),
        name="ffn_up",
    )(h, wg, wu)


def _ffn_down_kernel(a_ref, wd_ref, x_ref, gt_ref, g_ref, *rest, ctx_row, final):
    xn = x_ref[0] + _mod_row(gt_ref, ctx_row) * _dot(a_ref[0], wd_ref[...])
    if final:
        (o_ref,) = rest
        o_ref[0] = _rms(xn, g_ref[...])
    else:
        sc_ref, sh_ref, xo_ref, ho_ref = rest
        xo_ref[0] = xn
        ho_ref[0] = _modnorm(xn, g_ref[...], _mod_row(sc_ref, ctx_row),
                             _mod_row(sh_ref, ctx_row)).astype(BF16)


def _ffn_down(act, wd, x, g_next, mod, layer, ctx_row, final):
    b, t, d = x.shape
    dff = wd.shape[0]
    tm = min(TM_DOWN, t)
    row_spec = pl.BlockSpec((1, tm, d), lambda i, j: (i, j, 0))
    in_specs = [
        pl.BlockSpec((1, tm, dff), lambda i, j: (i, j, 0)),
        pl.BlockSpec((dff, d), lambda i, j: (0, 0), pipeline_mode=pl.Buffered(1)),
        row_spec,
        _mod_spec(layer, 5),
        _const_spec((1, d)),
    ]
    operands = [act, wd, x, mod, g_next.reshape(1, d)]
    if final:
        out_specs = row_spec
        out_shape = jax.ShapeDtypeStruct((b, t, d), F32)
    else:
        in_specs += [_mod_spec(layer + 1, 1), _mod_spec(layer + 1, 0)]
        operands += [mod, mod]
        out_specs = [row_spec, row_spec]
        out_shape = [jax.ShapeDtypeStruct((b, t, d), F32), jax.ShapeDtypeStruct((b, t, d), BF16)]
    return pl.pallas_call(
        functools.partial(_ffn_down_kernel, ctx_row=ctx_row, final=final),
        grid=(b, t // tm),
        in_specs=in_specs,
        out_specs=out_specs,
        out_shape=out_shape,
        compiler_params=_params("parallel", "parallel"),
        name="ffn_down",
    )(*operands)


def _ffn(h, wg, wu, wd, x, g_next, mod, layer, ctx_row, final):
    return _ffn_down(_ffn_up(h, wg, wu), wd, x, g_next, mod, layer, ctx_row, final)


def _rope_tables(s, n_ctx):
    t = jnp.arange(s, dtype=jnp.int32)
    row = (t // GRID_W).astype(F32)
    col = (t % GRID_W).astype(F32)

    def cos_sin(dim):
        d_axis = dim // 2
        inv = 1.0 / (ROPE_THETA ** (jnp.arange(0, d_axis, 2, dtype=F32) / d_axis))
        ang = jnp.concatenate([row[:, None] * inv, col[:, None] * inv], axis=-1)
        return jnp.cos(ang), jnp.sin(ang)

    c128, s128 = cos_sin(HEAD_DIM)
    c64, s64 = cos_sin(MLA_ROPE)
    z = jnp.zeros_like(s64)
    lat = (
        jnp.concatenate([c128, c128], axis=-1),
        jnp.concatenate([-s128, s128], axis=-1),
        jnp.tile(c64, (1, 4)),
        jnp.tile(jnp.concatenate([-s64, z], axis=-1), (1, 2)),
        jnp.tile(jnp.concatenate([z, s64], axis=-1), (1, 2)),
    )
    one = jnp.ones((n_ctx, 128), F32)
    zero = jnp.zeros((n_ctx, 128), F32)
    return lat, (one, zero, one, zero, zero)


def kernel(x, c, ctx, c_ctx, w_ada, b_ada, g_attn, g_ffn, w_in, w_out, na_rpb, gqa_gq, gqa_gk,
           mla_gq, mla_gkv, mla_wuq, mla_wukv, diff_lq1, diff_lk1, diff_lq2, diff_lk2, diff_gsub,
           ffn_wg, ffn_wu, ffn_wd, g_final):
    depth = w_ada.shape[0]
    b, s, d = x.shape
    n_ctx = ctx.shape[1]
    rows = s // GRID_W
    ctx_row = b

    mod = _modulation(c, c_ctx, w_ada, b_ada)
    tabs_lat, tabs_ctx = _rope_tables(s, n_ctx)

    xl, xc = x, ctx
    hl = _first_norm(xl, g_attn[0], mod, 0, None)
    hc = _first_norm(xc, g_attn[0], mod, 0, ctx_row)

    for l in range(depth):
        ctx_out = l < depth - 1
        lam_init = 0.8 - 0.6 * math.exp(-0.3 * l)
        wl = w_in[l].astype(BF16)
        w_a = wl[:, OFF_A:OFF_B]
        w_b = wl[:, OFF_B:OFF_C]
        w_c = jnp.pad(wl[:, OFF_C:OFF_D], ((0, 0), (0, 128 - MLA_ROPE)))
        w_d = wl[:, OFF_D:OFF_END]
        wuq = jnp.pad(mla_wuq[l].astype(BF16).reshape(MLA_Q_RANK, N_HEADS, MLA_NOPE + MLA_ROPE),
                      ((0, 0), (0, 0), (0, 256 - MLA_NOPE - MLA_ROPE))).reshape(MLA_Q_RANK, N_HEADS * 256)
        wukv = mla_wukv[l].astype(BF16).reshape(MLA_KV_RANK, N_HEADS, 2, 128)
        wukv = wukv.transpose(0, 2, 1, 3).reshape(MLA_KV_RANK, 2 * N_HEADS * 128)
        wo = w_out[l].astype(BF16)
        wg, wu, wd = ffn_wg[l].astype(BF16), ffn_wu[l].astype(BF16), ffn_wd[l].astype(BF16)
        gq, gk = gqa_gq[l].reshape(1, 128), gqa_gk[l].reshape(1, 128)
        mgq, mgkv = mla_gq[l].reshape(1, MLA_Q_RANK), mla_gkv[l].reshape(1, MLA_KV_RANK)
        bias = _na_bias(na_rpb[l], rows)
        diff = ((diff_lq1[l], diff_lk1[l], diff_lq2[l], diff_lk2[l]), diff_gsub[l], lam_init)

        def project(h, tabs):
            t128, t64 = tabs[:2], tabs[2:]
            pa = _in_proj(_in_a_kernel, "in_proj_na", h, [w_a], [], [], 1536)
            pb = _in_proj(_in_b_kernel, "in_proj_gqa", h, [w_b], [gq, gk], t128, 1024)
            pc = _in_proj(_in_c_kernel, "in_proj_mla", h, [w_c], [mgq, mgkv, wuq, wukv], t64, 2560)
            pd = _in_proj(_in_d_kernel, "in_proj_diff", h, [w_d], [], t64, 1536)
            return pa, pb, pc, pd

        pa, pb, pc, pd = project(hl, tabs_lat)
        pa_c, pb_c, pc_c, pd_c = project(hc, tabs_ctx)

        gqa_idx = dict(dk=128, q_idx=lambda h: h, k_idx=lambda h: 4 + h // 2, v_idx=lambda h: 6 + h // 2)
        mla_idx = dict(dk=256, q_idx=lambda h: h, k_idx=lambda h: 4 + h, v_idx=lambda h: 16 + h)
        std_idx = dict(dk=128, q_idx=lambda h: h, k_idx=lambda h: 4 + h, v_idx=lambda h: 8 + h)

        outs = (
            _na_attention(pa, pa_c, bias),
            _attention("gqa_attention", pb, pb, pb_c, **gqa_idx),
            _attention("mla_attention", pc, pc, pc_c, **mla_idx),
            _attention("diff_attention", pd, pd, pd_c, diff=diff, **std_idx),
        )
        xl, hl = _out_proj(outs, wo, xl, g_ffn[l], mod, l, None)
        if ctx_out:
            outs_c = (
                _attention("na_ctx_attention", pa_c, None, pa_c, **std_idx),
                _attention("gqa_ctx_attention", pb_c, None, pb_c, **gqa_idx),
                _attention("mla_ctx_attention", pc_c, None, pc_c, **mla_idx),
                _attention("diff_ctx_attention", pd_c, None, pd_c, diff=diff, **std_idx),
            )
            xc, hc = _out_proj(outs_c, wo, xc, g_ffn[l], mod, l, ctx_row)
            xl, hl = _ffn(hl, wg, wu, wd, xl, g_attn[l + 1], mod, l, None, False)
            xc, hc = _ffn(hc, wg, wu, wd, xc, g_attn[l + 1], mod, l, ctx_row, False)
        else:
            xl = _ffn(hl, wg, wu, wd, xl, g_final, mod, l, None, True)
    return xl
```

```python
import functools
import math

import numpy as np
import jax
import jax.numpy as jnp
from jax import lax
from jax.experimental import pallas as pl
from jax.experimental.pallas import tpu as pltpu

F32 = jnp.float32
BF16 = jnp.bfloat16

D_MODEL = 2048
GRID_W = 64
HEAD_DIM = 128
N_HEADS = 4
ROPE_THETA = 10000.0
EPS = 1e-6
NA_KH = 8
NA_KW = 16
NA_ROWS_PER_BLOCK = 4
NA_WIN_ROWS = 12
GQA_KV_HEADS = 2
MLA_Q_RANK = 512
MLA_KV_RANK = 256
MLA_NOPE = 128
MLA_ROPE = 64
DIFF_D = 64
D_FF = 5632

LOG2E = math.log2(math.e)
NA_SCALE = HEAD_DIM ** -0.5 * LOG2E
GQA_SCALE = HEAD_DIM ** -0.5 * LOG2E
MLA_SCALE = (MLA_NOPE + MLA_ROPE) ** -0.5 * LOG2E
DIFF_SCALE = DIFF_D ** -0.5 * LOG2E
NEG_BIG = -1e30

OFF_D = 3392

V7X_VMEM_LIMIT = 56 * 1024 * 1024

TQ = 512
HEADS_PER_STEP = 2
BK = 1024
TM_DENSE = 512
TM_UP = 1024
TM_DOWN = 256
TF = 512
ROW_SUB = 256


def _row_subtiles(tm):
    rs = min(ROW_SUB, tm)
    return [slice(r, r + rs) for r in range(0, tm, rs)]


def _params(*sem):
    return pltpu.CompilerParams(dimension_semantics=sem, vmem_limit_bytes=V7X_VMEM_LIMIT)


def _dot(a, b):
    return jnp.dot(a, b, preferred_element_type=F32)


def _dot_nt(a, b):
    return lax.dot_general(a, b, (((1,), (1,)), ((), ())), preferred_element_type=F32)


def _mod_row(ref, ctx_row):
    r = pl.program_id(0) if ctx_row is None else ctx_row
    return ref[0, 0, pl.ds(r, 1), :]


def _rms(x, g):
    return x * lax.rsqrt(jnp.mean(x * x, axis=-1, keepdims=True) + EPS) * g


def _modnorm(x, g, sc, sh):
    return _rms(x, g) * (1.0 + sc) + sh


def _rope128(x, cos, sin):
    return x * cos + pltpu.roll(x, 64, 1) * sin


def _rope64(x, c, sa, sb):
    return x * c + pltpu.roll(x, 96, 1) * sa + pltpu.roll(x, 32, 1) * sb


def _mod_kernel(cc_ref, w_ref, b_ref, o_ref):
    cc = cc_ref[...]
    s = cc / (1.0 + jnp.exp(-cc))
    o_ref[0, 0] = _dot(s.astype(BF16), w_ref[0].astype(BF16)) + b_ref[0]


def _modulation(c, c_ctx, w_ada, b_ada):
    depth, d, n = w_ada.shape
    b = c.shape[0]
    assert b < 8
    cc = jnp.zeros((8, d), F32).at[:b].set(c).at[b].set(c_ctx)
    tn = 1024
    per = d // tn
    return pl.pallas_call(
        _mod_kernel,
        grid=(depth, n // tn),
        in_specs=[
            pl.BlockSpec((8, d), lambda l, j: (0, 0)),
            pl.BlockSpec((1, d, tn), lambda l, j: (l, 0, j)),
            pl.BlockSpec((1, 1, tn), lambda l, j: (l, 0, j)),
        ],
        out_specs=pl.BlockSpec((1, 1, 8, tn), lambda l, j: (l, j // per, 0, j % per)),
        out_shape=jax.ShapeDtypeStruct((depth, n // d, 8, d), F32),
        compiler_params=_params("parallel", "parallel"),
        name="adaln_mod",
    )(cc, w_ada, b_ada.reshape(depth, 1, n))


def _mod_spec(layer, chunk):
    return pl.BlockSpec((1, 1, 8, D_MODEL), lambda *_: (layer, chunk, 0, 0))


def _norm_kernel(x_ref, g_ref, sc_ref, sh_ref, o_ref, *, ctx_row):
    o_ref[0] = _modnorm(x_ref[0], g_ref[...], _mod_row(sc_ref, ctx_row),
                        _mod_row(sh_ref, ctx_row)).astype(BF16)


def _first_norm(x, g, mod, layer, ctx_row):
    b, t, d = x.shape
    tm = min(TM_DENSE, t)
    return pl.pallas_call(
        functools.partial(_norm_kernel, ctx_row=ctx_row),
        grid=(b, t // tm),
        in_specs=[
            pl.BlockSpec((1, tm, d), lambda i, j: (i, j, 0)),
            pl.BlockSpec((1, d), lambda i, j: (0, 0)),
            _mod_spec(layer, 1),
            _mod_spec(layer, 0),
        ],
        out_specs=pl.BlockSpec((1, tm, d), lambda i, j: (i, j, 0)),
        out_shape=jax.ShapeDtypeStruct((b, t, d), BF16),
        compiler_params=_params("parallel", "parallel"),
        name="first_norm",
    )(x, g.reshape(1, d), mod, mod)


W_COLS = (0, 1536, 2560, 3456, 4992)
IN_PROJ_WIDTHS = (1536, 1024, 2560, 1536)


def _epi_na(y, rows, o_ref):
    o_ref[0, rows, :512] = (y[:, :512] * NA_SCALE).astype(BF16)
    o_ref[0, rows, 512:] = y[:, 512:].astype(BF16)


def _epi_gqa(y, rows, gq_ref, gk_ref, cos, sin, o_ref):
    for j in range(N_HEADS + GQA_KV_HEADS):
        g = gq_ref[...] if j < N_HEADS else gk_ref[...]
        r = _rope128(_rms(y[:, 128 * j:128 * (j + 1)], g), cos, sin)
        if j < N_HEADS:
            r = r * GQA_SCALE
        o_ref[0, rows, 128 * j:128 * (j + 1)] = r.astype(BF16)
    o_ref[0, rows, 768:] = y[:, 768:].astype(BF16)


def _epi_mla(y, rows, gq_ref, gkv_ref, wuq_ref, wukv_ref, t64, o_ref):
    q = _dot(_rms(y[:, :512], gq_ref[...]).astype(BF16), wuq_ref[...])
    kv = _dot(_rms(y[:, 512:768], gkv_ref[...]).astype(BF16), wukv_ref[...])
    kpe = _rope64(y[:, 768:896], *t64).astype(BF16)
    for h in range(N_HEADS):
        o_ref[0, rows, 256 * h:256 * h + 128] = (q[:, 256 * h:256 * h + 128] * MLA_SCALE).astype(BF16)
        qpe = _rope64(q[:, 256 * h + 128:256 * h + 256], *t64)
        o_ref[0, rows, 256 * h + 128:256 * h + 256] = (qpe * MLA_SCALE).astype(BF16)
        o_ref[0, rows, 1024 + 256 * h:1024 + 256 * h + 128] = kv[:, 128 * h:128 * (h + 1)].astype(BF16)
        o_ref[0, rows, 1024 + 256 * h + 128:1024 + 256 * h + 256] = kpe
    o_ref[0, rows, 2048:] = kv[:, 512:].astype(BF16)


def _epi_diff(y, rows, t64, o_ref):
    for j in range(2 * N_HEADS):
        r = _rope64(y[:, 128 * j:128 * (j + 1)], *t64)
        if j < N_HEADS:
            r = r * DIFF_SCALE
        o_ref[0, rows, 128 * j:128 * (j + 1)] = r.astype(BF16)
    o_ref[0, rows, 1024:] = y[:, 1024:].astype(BF16)


def _in_proj_kernel(h_ref, w_ref, gq_ref, gk_ref, mgq_ref, mgkv_ref, wuq_ref, wukv_ref,
                    cos_ref, sin_ref, c_ref, sa_ref, sb_ref, oa_ref, ob_ref, oc_ref, od_ref):
    a0, b0, c0, d0, end = W_COLS
    for rows in _row_subtiles(h_ref.shape[1]):
        h = h_ref[0, rows, :]
        t64 = (c_ref[rows, :], sa_ref[rows, :], sb_ref[rows, :])
        _epi_na(_dot(h, w_ref[0, :, a0:b0]), rows, oa_ref)
        _epi_gqa(_dot(h, w_ref[0, :, b0:c0]), rows, gq_ref, gk_ref, cos_ref[rows, :], sin_ref[rows, :], ob_ref)
        _epi_mla(_dot(h, w_ref[0, :, c0:d0]), rows, mgq_ref, mgkv_ref, wuq_ref, wukv_ref, t64, oc_ref)
        _epi_diff(_dot(h, w_ref[0, :, d0:end]), rows, t64, od_ref)


def _const_spec(shape):
    return pl.BlockSpec(shape, lambda *_: (0,) * len(shape))


def _in_proj(h, w_all, layer, small, tables):
    b, t, d = h.shape
    tm = min(TM_DENSE, t)
    w_spec = pl.BlockSpec((1,) + w_all.shape[1:], lambda i, j: (layer, 0, 0), pipeline_mode=pl.Buffered(1))
    in_specs = [pl.BlockSpec((1, tm, d), lambda i, j: (i, j, 0)), w_spec]
    in_specs += [_const_spec(s.shape) for s in small]
    in_specs += [pl.BlockSpec((tm, 128), lambda i, j: (j, 0)) for _ in tables]
    return pl.pallas_call(
        _in_proj_kernel,
        grid=(b, t // tm),
        in_specs=in_specs,
        out_specs=[pl.BlockSpec((1, tm, n), lambda i, j: (i, j, 0)) for n in IN_PROJ_WIDTHS],
        out_shape=[jax.ShapeDtypeStruct((b, t, n), BF16) for n in IN_PROJ_WIDTHS],
        compiler_params=_params("parallel", "parallel"),
        name="in_proj",
    )(h, w_all, *small, *tables)


def _na_index_tables(rows):
    nblk = rows // NA_ROWS_PER_BLOCK
    dr = np.zeros((3, NA_ROWS_PER_BLOCK, NA_WIN_ROWS), np.int32)
    rv = np.zeros((3, NA_ROWS_PER_BLOCK, NA_WIN_ROWS), bool)
    for ty, jb in enumerate((0, 1, nblk - 1)):
        ws = NA_ROWS_PER_BLOCK * min(max(jb - 1, 0), nblk - 3)
        for a in range(NA_ROWS_PER_BLOCK):
            r = NA_ROWS_PER_BLOCK * jb + a
            r_start = min(max(r - NA_KH // 2, 0), rows - NA_KH)
            for kap in range(NA_WIN_ROWS):
                kr = ws + kap
                ok = r_start <= kr < r_start + NA_KH
                rv[ty, a, kap] = ok
                dr[ty, a, kap] = (kr - r + NA_KH - 1) if ok else 0
    cols = np.arange(GRID_W)
    c_start = np.clip(cols - NA_KW // 2, 0, GRID_W - NA_KW)
    kc = np.arange(GRID_W)[None, :]
    cv = (kc >= c_start[:, None]) & (kc < c_start[:, None] + NA_KW)
    dc = np.where(cv, kc - cols[:, None] + NA_KW - 1, 0).astype(np.int32)
    return dr, rv, dc, cv


def _na_bias(rpb, rows):
    dr, rv, dc, cv = _na_index_tables(rows)
    nh, _, ndc = rpb.shape
    onehot = np.zeros((ndc + 1, GRID_W, GRID_W), np.float32)
    onehot[np.where(cv, dc, ndc), np.arange(GRID_W)[:, None], np.arange(GRID_W)[None, :]] = 1.0
    r = jnp.take(rpb, jnp.asarray(dr.reshape(-1)), axis=1) * LOG2E
    r = r.reshape(nh, 3, NA_ROWS_PER_BLOCK, NA_WIN_ROWS, ndc)
    r = jnp.where(jnp.asarray(rv)[None, ..., None], r, NEG_BIG)
    r = jnp.concatenate([r, jnp.full(r.shape[:-1] + (1,), NEG_BIG, F32)], axis=-1)
    full = jnp.einsum("htakd,dcj->htackj", r, jnp.asarray(onehot), precision=lax.Precision.HIGHEST)
    return full.reshape(nh, 3, NA_ROWS_PER_BLOCK * GRID_W, NA_WIN_ROWS * GRID_W)


def _lane_fold(x, op):
    acc = x[:, :128]
    for j in range(1, x.shape[1] // 128):
        acc = op(acc, x[:, 128 * j:128 * (j + 1)])
    return acc


def _online_update(state, s, v):
    v1 = jnp.concatenate([v, jnp.ones_like(v)], axis=1)
    mc = jnp.max(_lane_fold(s, jnp.maximum), axis=-1, keepdims=True)
    if state is None:
        return mc, _dot(jnp.exp2(s - mc).astype(BF16), v1)
    m, acc = state
    m_new = jnp.maximum(m, mc)
    p = jnp.exp2(s - m_new)
    return m_new, acc * jnp.exp2(m - m_new) + _dot(p.astype(BF16), v1)


def _softmax_finish(state):
    _, acc = state
    return acc[:, :128] / acc[:, 128:]


def _na_kernel(q_ref, k_ref, v_ref, kc_ref, vc_ref, bias_ref, o_ref, *, nblk):
    jb = pl.program_id(1)
    start = pl.multiple_of(jnp.clip(jb - 1, 0, nblk - 3) * 256, 256)
    nwin = NA_WIN_ROWS * GRID_W
    for h in range(N_HEADS):
        cols = slice(128 * h, 128 * (h + 1))
        q = q_ref[0, :, cols]
        s_w = _dot_nt(q, k_ref[0, pl.ds(start, nwin), cols]) + bias_ref[h, 0]
        state = _online_update(None, s_w, v_ref[0, pl.ds(start, nwin), cols])
        state = _online_update(state, _dot_nt(q, kc_ref[0, :, cols]), vc_ref[0, :, cols])
        o_ref[0, :, cols] = _softmax_finish(state).astype(BF16)


def _na_attention(pa, pa_ctx, bias):
    b, s, _ = pa.shape
    c = pa_ctx.shape[1]
    rows = s // GRID_W
    nblk = rows // NA_ROWS_PER_BLOCK
    tq = NA_ROWS_PER_BLOCK * GRID_W
    nwin = NA_WIN_ROWS * GRID_W
    hw = N_HEADS * 128

    def bias_map(i, j):
        ty = (j > 0).astype(jnp.int32) + (j == nblk - 1).astype(jnp.int32)
        return (0, ty, 0, 0)

    return pl.pallas_call(
        functools.partial(_na_kernel, nblk=nblk),
        grid=(b, nblk),
        in_specs=[
            pl.BlockSpec((1, tq, hw), lambda i, j: (i, j, 0)),
            pl.BlockSpec((1, s, hw), lambda i, j: (i, 0, 1)),
            pl.BlockSpec((1, s, hw), lambda i, j: (i, 0, 2)),
            pl.BlockSpec((1, c, hw), lambda i, j: (i, 0, 1)),
            pl.BlockSpec((1, c, hw), lambda i, j: (i, 0, 2)),
            pl.BlockSpec((N_HEADS, 1, tq, nwin), bias_map),
        ],
        out_specs=pl.BlockSpec((1, tq, hw), lambda i, j: (i, j, 0)),
        out_shape=jax.ShapeDtypeStruct((b, s, hw), BF16),
        compiler_params=_params("parallel", "arbitrary"),
        name="na_attention",
    )(pa, pa, pa, pa_ctx, pa_ctx, bias)


def _key_chunks(krefs, vrefs):
    chunks = []
    for k_ref, v_ref in zip(krefs, vrefs):
        n = k_ref.shape[1]
        chunks += [(k_ref, v_ref, st, min(BK, n - st)) for st in range(0, n, BK)]
    return chunks


def _head_cols(ref, h, width):
    return slice(0, width) if ref.shape[2] == width else slice(h * width, (h + 1) * width)


def _attn_kernel(q_ref, *refs, dk):
    o_ref = refs[-1]
    chunks = _key_chunks(refs[0:-1:2], refs[1:-1:2])
    states = [None] * HEADS_PER_STEP
    for k_ref, v_ref, st, sz in chunks:
        for h in range(HEADS_PER_STEP):
            q = q_ref[0, :, h * dk:(h + 1) * dk]
            k = k_ref[0, st:st + sz, _head_cols(k_ref, h, dk)]
            v = v_ref[0, st:st + sz, _head_cols(v_ref, h, 128)]
            states[h] = _online_update(states[h], _dot_nt(q, k), v)
    for h in range(HEADS_PER_STEP):
        o_ref[0, :, 128 * h:128 * (h + 1)] = _softmax_finish(states[h]).astype(BF16)


def _diff_kernel(q_ref, *refs, lam_init):
    lq1_ref, lk1_ref, lq2_ref, lk2_ref, g_ref, o_ref = refs[-6:]
    chunks = _key_chunks(refs[0:-6:2], refs[1:-6:2])
    lam = (jnp.exp(jnp.sum(lq1_ref[...] * lk1_ref[...], axis=-1, keepdims=True))
           - jnp.exp(jnp.sum(lq2_ref[...] * lk2_ref[...], axis=-1, keepdims=True)) + lam_init)
    lane = lax.broadcasted_iota(jnp.int32, (1, 128), 1)
    low = (lane < DIFF_D).astype(BF16)
    qs = []
    for h in range(HEADS_PER_STEP):
        q = q_ref[0, :, 128 * h:128 * (h + 1)]
        qs += [q * low, q * (1.0 - low).astype(BF16)]
    states = [None] * len(qs)
    for k_ref, v_ref, st, sz in chunks:
        for i, q in enumerate(qs):
            cols = slice(128 * (i // 2), 128 * (i // 2 + 1))
            states[i] = _online_update(states[i], _dot_nt(q, k_ref[0, st:st + sz, cols]),
                                       v_ref[0, st:st + sz, cols])
    for h in range(HEADS_PER_STEP):
        o = _softmax_finish(states[2 * h]) - lam * _softmax_finish(states[2 * h + 1])
        o_ref[0, :, 128 * h:128 * (h + 1)] = (_rms(o, g_ref[...]) * (1.0 - lam_init)).astype(BF16)


def _attention(name, pq, p_lat, p_ctx, *, dk, k_off, v_off, shared_kv=False, diff=None):
    b, t, _ = pq.shape
    tq = min(TQ, t)
    hp = HEADS_PER_STEP
    kw, vw = (dk, 128) if shared_kv else (hp * dk, hp * 128)
    in_specs = [pl.BlockSpec((1, tq, hp * dk), lambda i, g, j: (i, j, g))]
    operands = [pq]
    for p in (p_lat, p_ctx):
        if p is None:
            continue
        n = p.shape[1]
        in_specs.append(pl.BlockSpec((1, n, kw), lambda i, g, j: (i, 0, k_off // kw + g)))
        in_specs.append(pl.BlockSpec((1, n, vw), lambda i, g, j: (i, 0, v_off // vw + g)))
        operands += [p, p]
    if diff is None:
        kernel = functools.partial(_attn_kernel, dk=dk)
    else:
        lam_vecs, g_sub, lam_init = diff
        kernel = functools.partial(_diff_kernel, lam_init=lam_init)
        in_specs += [_const_spec((1, DIFF_D))] * 4 + [_const_spec((1, 128))]
        operands += [v.reshape(1, DIFF_D) for v in lam_vecs] + [g_sub.reshape(1, 128)]
    return pl.pallas_call(
        kernel,
        grid=(b, N_HEADS // hp, t // tq),
        in_specs=in_specs,
        out_specs=pl.BlockSpec((1, tq, hp * 128), lambda i, g, j: (i, j, g)),
        out_shape=jax.ShapeDtypeStruct((b, t, N_HEADS * 128), BF16),
        compiler_params=_params("parallel", "parallel", "arbitrary"),
        name=name,
    )(*operands)


def _out_kernel(oa_ref, ob_ref, oc_ref, od_ref, w_ref, x_ref, gt_ref, g_ref, sc_ref, sh_ref,
                xo_ref, ho_ref, *, ctx_row):
    gate, sc, sh = _mod_row(gt_ref, ctx_row), _mod_row(sc_ref, ctx_row), _mod_row(sh_ref, ctx_row)
    for rows in _row_subtiles(x_ref.shape[1]):
        y = _dot(oa_ref[0, rows, :], w_ref[0:512, :])
        y = y + _dot(ob_ref[0, rows, :], w_ref[512:1024, :])
        y = y + _dot(oc_ref[0, rows, :], w_ref[1024:1536, :])
        y = y + _dot(od_ref[0, rows, :], w_ref[1536:2048, :])
        xn = x_ref[0, rows, :] + gate * y
        xo_ref[0, rows, :] = xn
        ho_ref[0, rows, :] = _modnorm(xn, g_ref[...], sc, sh).astype(BF16)


def _out_proj(outs, w_out, x, g_ffn, mod, layer, ctx_row):
    b, t, d = x.shape
    tm = min(TM_DENSE, t)
    o_spec = pl.BlockSpec((1, tm, 512), lambda i, j: (i, j, 0))
    x_spec = pl.BlockSpec((1, tm, d), lambda i, j: (i, j, 0))
    return pl.pallas_call(
        functools.partial(_out_kernel, ctx_row=ctx_row),
        grid=(b, t // tm),
        in_specs=[o_spec] * 4 + [
            _const_spec(w_out.shape),
            x_spec,
            _mod_spec(layer, 2),
            _const_spec((1, d)),
            _mod_spec(layer, 4),
            _mod_spec(layer, 3),
        ],
        out_specs=[x_spec, x_spec],
        out_shape=[jax.ShapeDtypeStruct((b, t, d), F32), jax.ShapeDtypeStruct((b, t, d), BF16)],
        compiler_params=_params("parallel", "parallel"),
        name="out_proj",
    )(*outs, w_out, x, mod, g_ffn.reshape(1, d), mod, mod)


def _ffn_up_kernel(h_ref, wg_ref, wu_ref, o_ref, wgb_ref, wub_ref):
    @pl.when((pl.program_id(1) == 0) & (pl.program_id(2) == 0))
    def _():
        wgb_ref[...] = wg_ref[0].astype(BF16)
        wub_ref[...] = wu_ref[0].astype(BF16)

    for rows in _row_subtiles(h_ref.shape[1]):
        h = h_ref[0, rows, :]
        gate = _dot(h, wgb_ref[...])
        up = _dot(h, wub_ref[...])
        o_ref[0, rows, :] = ((gate / (1.0 + jnp.exp(-gate))) * up).astype(BF16)


def _ffn_up(h, wg, wu, layer):
    b, t, d = h.shape
    dff = wg.shape[2]
    tm = min(TM_UP, t)
    w_spec = pl.BlockSpec((1, d, TF), lambda f, i, j: (layer, 0, f))
    return pl.pallas_call(
        _ffn_up_kernel,
        grid=(dff // TF, b, t // tm),
        in_specs=[pl.BlockSpec((1, tm, d), lambda f, i, j: (i, j, 0)), w_spec, w_spec],
        out_specs=pl.BlockSpec((1, tm, TF), lambda f, i, j: (i, j, f)),
        out_shape=jax.ShapeDtypeStruct((b, t, dff), BF16),
        scratch_shapes=[pltpu.VMEM((d, TF), BF16), pltpu.VMEM((d, TF), BF16)],
        compiler_params=_params("arbitrary", "arbitrary", "arbitrary"),
        name="ffn_up",
    )(h, wg, wu)


def _ffn_down_kernel(a_ref, wd_ref, x_ref, gt_ref, g_ref, *rest, ctx_row, final):
    xn = x_ref[0] + _mod_row(gt_ref, ctx_row) * _dot(a_ref[0], wd_ref[...])
    if final:
        (o_ref,) = rest
        o_ref[0] = _rms(xn, g_ref[...])
    else:
        sc_ref, sh_ref, xo_ref, ho_ref = rest
        xo_ref[0] = xn
        ho_ref[0] = _modnorm(xn, g_ref[...], _mod_row(sc_ref, ctx_row),
                             _mod_row(sh_ref, ctx_row)).astype(BF16)


def _ffn_down(act, wd, x, g_next, mod, layer, ctx_row, final):
    b, t, d = x.shape
    dff = wd.shape[0]
    tm = min(TM_DOWN, t)
    row_spec = pl.BlockSpec((1, tm, d), lambda i, j: (i, j, 0))
    in_specs = [
        pl.BlockSpec((1, tm, dff), lambda i, j: (i, j, 0)),
        pl.BlockSpec((dff, d), lambda i, j: (0, 0), pipeline_mode=pl.Buffered(1)),
        row_spec,
        _mod_spec(layer, 5),
        _const_spec((1, d)),
    ]
    operands = [act, wd, x, mod, g_next.reshape(1, d)]
    if final:
        out_specs = row_spec
        out_shape = jax.ShapeDtypeStruct((b, t, d), F32)
    else:
        in_specs += [_mod_spec(layer + 1, 1), _mod_spec(layer + 1, 0)]
        operands += [mod, mod]
        out_specs = [row_spec, row_spec]
        out_shape = [jax.ShapeDtypeStruct((b, t, d), F32), jax.ShapeDtypeStruct((b, t, d), BF16)]
    return pl.pallas_call(
        functools.partial(_ffn_down_kernel, ctx_row=ctx_row, final=final),
        grid=(b, t // tm),
        in_specs=in_specs,
        out_specs=out_specs,
        out_shape=out_shape,
        compiler_params=_params("parallel", "parallel"),
        name="ffn_down",
    )(*operands)


def _ffn(h, wg, wu, wd, x, g_next, mod, layer, ctx_row, final):
    return _ffn_down(_ffn_up(h, wg, wu, layer), wd, x, g_next, mod, layer, ctx_row, final)


def _rope_tables(s, n_ctx):
    t = jnp.arange(s, dtype=jnp.int32)
    row = (t // GRID_W).astype(F32)
    col = (t % GRID_W).astype(F32)

    def cos_sin(dim):
        d_axis = dim // 2
        inv = 1.0 / (ROPE_THETA ** (jnp.arange(0, d_axis, 2, dtype=F32) / d_axis))
        ang = jnp.concatenate([row[:, None] * inv, col[:, None] * inv], axis=-1)
        return jnp.cos(ang), jnp.sin(ang)

    c128, s128 = cos_sin(HEAD_DIM)
    c64, s64 = cos_sin(MLA_ROPE)
    z = jnp.zeros_like(s64)
    lat = (
        jnp.concatenate([c128, c128], axis=-1),
        jnp.concatenate([-s128, s128], axis=-1),
        jnp.tile(c64, (1, 4)),
        jnp.tile(jnp.concatenate([-s64, z], axis=-1), (1, 2)),
        jnp.tile(jnp.concatenate([z, s64], axis=-1), (1, 2)),
    )
    one = jnp.ones((n_ctx, 128), F32)
    zero = jnp.zeros((n_ctx, 128), F32)
    return lat, (one, zero, one, zero, zero)


def kernel(x, c, ctx, c_ctx, w_ada, b_ada, g_attn, g_ffn, w_in, w_out, na_rpb, gqa_gq, gqa_gk,
           mla_gq, mla_gkv, mla_wuq, mla_wukv, diff_lq1, diff_lk1, diff_lq2, diff_lk2, diff_gsub,
           ffn_wg, ffn_wu, ffn_wd, g_final):
    depth = w_ada.shape[0]
    b, s, d = x.shape
    n_ctx = ctx.shape[1]
    rows = s // GRID_W
    ctx_row = b

    mod = _modulation(c, c_ctx, w_ada, b_ada)
    w_pad = jnp.zeros(w_in.shape[:2] + (128 - MLA_ROPE,), w_in.dtype)
    w_all = jnp.concatenate([w_in[:, :, :OFF_D], w_pad, w_in[:, :, OFF_D:]], axis=2).astype(BF16)
    tabs_lat, tabs_ctx = _rope_tables(s, b * n_ctx)

    def per_sample(a):
        return a.reshape(b, n_ctx, a.shape[-1])

    def flat(a):
        return a.reshape(1, b * n_ctx, a.shape[-1])

    xl, xc = x, flat(ctx)
    hl = _first_norm(xl, g_attn[0], mod, 0, None)
    hc = _first_norm(xc, g_attn[0], mod, 0, ctx_row)

    for l in range(depth):
        ctx_out = l < depth - 1
        lam_init = 0.8 - 0.6 * math.exp(-0.3 * l)
        wuq = jnp.pad(mla_wuq[l].astype(BF16).reshape(MLA_Q_RANK, N_HEADS, MLA_NOPE + MLA_ROPE),
                      ((0, 0), (0, 0), (0, 256 - MLA_NOPE - MLA_ROPE))).reshape(MLA_Q_RANK, N_HEADS * 256)
        wukv = mla_wukv[l].astype(BF16).reshape(MLA_KV_RANK, N_HEADS, 2, 128)
        wukv = wukv.transpose(0, 2, 1, 3).reshape(MLA_KV_RANK, 2 * N_HEADS * 128)
        wo = w_out[l].astype(BF16)
        wg, wu, wd = ffn_wg, ffn_wu, ffn_wd[l].astype(BF16)
        gq, gk = gqa_gq[l].reshape(1, 128), gqa_gk[l].reshape(1, 128)
        mgq, mgkv = mla_gq[l].reshape(1, MLA_Q_RANK), mla_gkv[l].reshape(1, MLA_KV_RANK)
        bias = _na_bias(na_rpb[l], rows)
        diff = ((diff_lq1[l], diff_lk1[l], diff_lq2[l], diff_lk2[l]), diff_gsub[l], lam_init)

        small = [gq, gk, mgq, mgkv, wuq, wukv]
        pa, pb, pc, pd = _in_proj(hl, w_all, l, small, tabs_lat)
        pa_c, pb_c, pc_c, pd_c = (per_sample(p) for p in _in_proj(hc, w_all, l, small, tabs_ctx))

        gqa_idx = dict(dk=128, k_off=512, v_off=768, shared_kv=True)
        mla_idx = dict(dk=256, k_off=1024, v_off=2048)
        std_idx = dict(dk=128, k_off=512, v_off=1024)

        outs = (
            _na_attention(pa, pa_c, bias),
            _attention("gqa_attention", pb, pb, pb_c, **gqa_idx),
            _attention("mla_attention", pc, pc, pc_c, **mla_idx),
            _attention("diff_attention", pd, pd, pd_c, diff=diff, **std_idx),
        )
        xl, hl = _out_proj(outs, wo, xl, g_ffn[l], mod, l, None)
        if ctx_out:
            outs_c = (
                _attention("na_ctx_attention", pa_c, None, pa_c, **std_idx),
                _attention("gqa_ctx_attention", pb_c, None, pb_c, **gqa_idx),
                _attention("mla_ctx_attention", pc_c, None, pc_c, **mla_idx),
                _attention("diff_ctx_attention", pd_c, None, pd_c, diff=diff, **std_idx),
            )
            xc, hc = _out_proj([flat(o) for o in outs_c], wo, xc, g_ffn[l], mod, l, ctx_row)
            xl, hl = _ffn(hl, wg, wu, wd, xl, g_attn[l + 1], mod, l, None, False)
            xc, hc = _ffn(hc, wg, wu, wd, xc, g_attn[l + 1], mod, l, ctx_row, False)
        else:
            xl = _ffn(hl, wg, wu, wd, xl, g_final, mod, l, None, True)
    return xl
```

```python
import functools
import math

import numpy as np
import jax
import jax.numpy as jnp
from jax import lax
from jax.experimental import pallas as pl
from jax.experimental.pallas import tpu as pltpu

F32 = jnp.float32
BF16 = jnp.bfloat16

D_MODEL = 2048
GRID_W = 64
HEAD_DIM = 128
N_HEADS = 4
ROPE_THETA = 10000.0
EPS = 1e-6
NA_KH = 8
NA_KW = 16
NA_ROWS_PER_BLOCK = 4
NA_WIN_ROWS = 12
GQA_KV_HEADS = 2
MLA_Q_RANK = 512
MLA_KV_RANK = 256
MLA_NOPE = 128
MLA_ROPE = 64
DIFF_D = 64
D_FF = 5632

LOG2E = math.log2(math.e)
NA_SCALE = HEAD_DIM ** -0.5 * LOG2E
GQA_SCALE = HEAD_DIM ** -0.5 * LOG2E
MLA_SCALE = (MLA_NOPE + MLA_ROPE) ** -0.5 * LOG2E
DIFF_SCALE = DIFF_D ** -0.5 * LOG2E
NEG_BIG = -1e30

OFF_D = 3392

V7X_VMEM_LIMIT = 56 * 1024 * 1024

TQ = 512
HEADS_PER_STEP = 2
BK = 1024
TM_DENSE = 512
TM_UP = 2048
TM_DOWN = 256
TF = 512
ROW_SUB = 256


def _row_subtiles(tm):
    rs = min(ROW_SUB, tm)
    return [slice(r, r + rs) for r in range(0, tm, rs)]


def _params(*sem):
    return pltpu.CompilerParams(dimension_semantics=sem, vmem_limit_bytes=V7X_VMEM_LIMIT)


def _dot(a, b):
    return jnp.dot(a, b, preferred_element_type=F32)


def _dot_nt(a, b):
    return lax.dot_general(a, b, (((1,), (1,)), ((), ())), preferred_element_type=F32)


def _mod_row(ref, ctx_row):
    r = pl.program_id(0) if ctx_row is None else ctx_row
    return ref[0, 0, pl.ds(r, 1), :]


def _rms(x, g):
    return x * lax.rsqrt(jnp.mean(x * x, axis=-1, keepdims=True) + EPS) * g


def _modnorm(x, g, sc, sh):
    return _rms(x, g) * (1.0 + sc) + sh


def _rope128(x, cos, sin):
    return x * cos + pltpu.roll(x, 64, 1) * sin


def _rope64(x, c, sa, sb):
    return x * c + pltpu.roll(x, 96, 1) * sa + pltpu.roll(x, 32, 1) * sb


def _mod_kernel(cc_ref, w_ref, b_ref, o_ref):
    cc = cc_ref[...]
    s = cc / (1.0 + jnp.exp(-cc))
    o_ref[0, 0] = _dot(s.astype(BF16), w_ref[0].astype(BF16)) + b_ref[0]


def _modulation(c, c_ctx, w_ada, b_ada):
    depth, d, n = w_ada.shape
    b = c.shape[0]
    assert b < 8
    cc = jnp.zeros((8, d), F32).at[:b].set(c).at[b].set(c_ctx)
    tn = 1024
    per = d // tn
    return pl.pallas_call(
        _mod_kernel,
        grid=(depth, n // tn),
        in_specs=[
            pl.BlockSpec((8, d), lambda l, j: (0, 0)),
            pl.BlockSpec((1, d, tn), lambda l, j: (l, 0, j)),
            pl.BlockSpec((1, 1, tn), lambda l, j: (l, 0, j)),
        ],
        out_specs=pl.BlockSpec((1, 1, 8, tn), lambda l, j: (l, j // per, 0, j % per)),
        out_shape=jax.ShapeDtypeStruct((depth, n // d, 8, d), F32),
        compiler_params=_params("parallel", "parallel"),
        name="adaln_mod",
    )(cc, w_ada, b_ada.reshape(depth, 1, n))


def _mod_spec(layer, chunk):
    return pl.BlockSpec((1, 1, 8, D_MODEL), lambda *_: (layer, chunk, 0, 0))


W_COLS = (0, 1536, 2560, 3456, 4992)
IN_PROJ_WIDTHS = (1536, 1024, 2560, 1536)


def _epi_na(y, rows, o_ref):
    o_ref[0, rows, :512] = (y[:, :512] * NA_SCALE).astype(BF16)
    o_ref[0, rows, 512:] = y[:, 512:].astype(BF16)


def _epi_gqa(y, rows, gq_ref, gk_ref, cos, sin, o_ref):
    for j in range(N_HEADS + GQA_KV_HEADS):
        g = gq_ref[...] if j < N_HEADS else gk_ref[...]
        r = _rope128(_rms(y[:, 128 * j:128 * (j + 1)], g), cos, sin)
        if j < N_HEADS:
            r = r * GQA_SCALE
        o_ref[0, rows, 128 * j:128 * (j + 1)] = r.astype(BF16)
    o_ref[0, rows, 768:] = y[:, 768:].astype(BF16)


def _epi_mla(y, rows, gq_ref, gkv_ref, wuq_ref, wukv_ref, t64, o_ref):
    q = _dot(_rms(y[:, :512], gq_ref[...]).astype(BF16), wuq_ref[...])
    kv = _dot(_rms(y[:, 512:768], gkv_ref[...]).astype(BF16), wukv_ref[...])
    kpe = _rope64(y[:, 768:896], *t64).astype(BF16)
    for h in range(N_HEADS):
        o_ref[0, rows, 256 * h:256 * h + 128] = (q[:, 256 * h:256 * h + 128] * MLA_SCALE).astype(BF16)
        qpe = _rope64(q[:, 256 * h + 128:256 * h + 256], *t64)
        o_ref[0, rows, 256 * h + 128:256 * h + 256] = (qpe * MLA_SCALE).astype(BF16)
        o_ref[0, rows, 1024 + 256 * h:1024 + 256 * h + 128] = kv[:, 128 * h:128 * (h + 1)].astype(BF16)
        o_ref[0, rows, 1024 + 256 * h + 128:1024 + 256 * h + 256] = kpe
    o_ref[0, rows, 2048:] = kv[:, 512:].astype(BF16)


def _epi_diff(y, rows, t64, o_ref):
    for j in range(2 * N_HEADS):
        r = _rope64(y[:, 128 * j:128 * (j + 1)], *t64)
        if j < N_HEADS:
            r = r * DIFF_SCALE
        o_ref[0, rows, 128 * j:128 * (j + 1)] = r.astype(BF16)
    o_ref[0, rows, 1024:] = y[:, 1024:].astype(BF16)


def _in_proj_kernel(*refs, norm, ctx_row):
    if norm:
        h_ref, g_ref, sc_ref, sh_ref, *refs = refs
        g, sc, sh = g_ref[...], _mod_row(sc_ref, ctx_row), _mod_row(sh_ref, ctx_row)
    else:
        h_ref, *refs = refs
    (w_ref, gq_ref, gk_ref, mgq_ref, mgkv_ref, wuq_ref, wukv_ref,
     cos_ref, sin_ref, c_ref, sa_ref, sb_ref, oa_ref, ob_ref, oc_ref, od_ref) = refs
    a0, b0, c0, d0, end = W_COLS
    for rows in _row_subtiles(h_ref.shape[1]):
        h = h_ref[0, rows, :]
        if norm:
            h = _modnorm(h, g, sc, sh).astype(BF16)
        t64 = (c_ref[rows, :], sa_ref[rows, :], sb_ref[rows, :])
        _epi_na(_dot(h, w_ref[0, :, a0:b0]), rows, oa_ref)
        _epi_gqa(_dot(h, w_ref[0, :, b0:c0]), rows, gq_ref, gk_ref, cos_ref[rows, :], sin_ref[rows, :], ob_ref)
        _epi_mla(_dot(h, w_ref[0, :, c0:d0]), rows, mgq_ref, mgkv_ref, wuq_ref, wukv_ref, t64, oc_ref)
        _epi_diff(_dot(h, w_ref[0, :, d0:end]), rows, t64, od_ref)


def _const_spec(shape):
    return pl.BlockSpec(shape, lambda *_: (0,) * len(shape))


def _in_proj(h, w_all, layer, small, tables, norm=None, ctx_row=None):
    b, t, d = h.shape
    tm = min(TM_DENSE, t)
    w_spec = pl.BlockSpec((1,) + w_all.shape[1:], lambda i, j: (layer, 0, 0), pipeline_mode=pl.Buffered(1))
    in_specs = [pl.BlockSpec((1, tm, d), lambda i, j: (i, j, 0))]
    operands = [h]
    if norm is not None:
        g, mod = norm
        in_specs += [_const_spec((1, d)), _mod_spec(layer, 1), _mod_spec(layer, 0)]
        operands += [g.reshape(1, d), mod, mod]
    in_specs += [w_spec] + [_const_spec(s.shape) for s in small]
    in_specs += [pl.BlockSpec((tm, 128), lambda i, j: (j, 0)) for _ in tables]
    return pl.pallas_call(
        functools.partial(_in_proj_kernel, norm=norm is not None, ctx_row=ctx_row),
        grid=(b, t // tm),
        in_specs=in_specs,
        out_specs=[pl.BlockSpec((1, tm, n), lambda i, j: (i, j, 0)) for n in IN_PROJ_WIDTHS],
        out_shape=[jax.ShapeDtypeStruct((b, t, n), BF16) for n in IN_PROJ_WIDTHS],
        compiler_params=_params("parallel", "parallel"),
        name="in_proj",
    )(*operands, w_all, *small, *tables)


def _na_index_tables(rows):
    nblk = rows // NA_ROWS_PER_BLOCK
    dr = np.zeros((3, NA_ROWS_PER_BLOCK, NA_WIN_ROWS), np.int32)
    rv = np.zeros((3, NA_ROWS_PER_BLOCK, NA_WIN_ROWS), bool)
    for ty, jb in enumerate((0, 1, nblk - 1)):
        ws = NA_ROWS_PER_BLOCK * min(max(jb - 1, 0), nblk - 3)
        for a in range(NA_ROWS_PER_BLOCK):
            r = NA_ROWS_PER_BLOCK * jb + a
            r_start = min(max(r - NA_KH // 2, 0), rows - NA_KH)
            for kap in range(NA_WIN_ROWS):
                kr = ws + kap
                ok = r_start <= kr < r_start + NA_KH
                rv[ty, a, kap] = ok
                dr[ty, a, kap] = (kr - r + NA_KH - 1) if ok else 0
    cols = np.arange(GRID_W)
    c_start = np.clip(cols - NA_KW // 2, 0, GRID_W - NA_KW)
    kc = np.arange(GRID_W)[None, :]
    cv = (kc >= c_start[:, None]) & (kc < c_start[:, None] + NA_KW)
    dc = np.where(cv, kc - cols[:, None] + NA_KW - 1, 0).astype(np.int32)
    return dr, rv, dc, cv


def _na_bias(rpb, rows):
    dr, rv, dc, cv = _na_index_tables(rows)
    nh, _, ndc = rpb.shape
    onehot = np.zeros((ndc + 1, GRID_W, GRID_W), np.float32)
    onehot[np.where(cv, dc, ndc), np.arange(GRID_W)[:, None], np.arange(GRID_W)[None, :]] = 1.0
    r = jnp.take(rpb, jnp.asarray(dr.reshape(-1)), axis=1) * LOG2E
    r = r.reshape(nh, 3, NA_ROWS_PER_BLOCK, NA_WIN_ROWS, ndc)
    r = jnp.where(jnp.asarray(rv)[None, ..., None], r, NEG_BIG)
    r = jnp.concatenate([r, jnp.full(r.shape[:-1] + (1,), NEG_BIG, F32)], axis=-1)
    full = jnp.einsum("htakd,dcj->htackj", r, jnp.asarray(onehot), precision=lax.Precision.HIGHEST)
    return full.reshape(nh, 3, NA_ROWS_PER_BLOCK * GRID_W, NA_WIN_ROWS * GRID_W)


def _lane_fold(x, op):
    acc = x[:, :128]
    for j in range(1, x.shape[1] // 128):
        acc = op(acc, x[:, 128 * j:128 * (j + 1)])
    return acc


def _online_update(state, s, v):
    v1 = jnp.concatenate([v, jnp.ones_like(v)], axis=1)
    mc = jnp.max(_lane_fold(s, jnp.maximum), axis=-1, keepdims=True)
    if state is None:
        return mc, _dot(jnp.exp2(s - mc).astype(BF16), v1)
    m, acc = state
    m_new = jnp.maximum(m, mc)
    p = jnp.exp2(s - m_new)
    return m_new, acc * jnp.exp2(m - m_new) + _dot(p.astype(BF16), v1)


def _softmax_finish(state):
    _, acc = state
    return acc[:, :128] / acc[:, 128:]


def _na_kernel(q_ref, k_ref, v_ref, kc_ref, vc_ref, bias_ref, o_ref, *, nblk):
    jb = pl.program_id(1)
    start = pl.multiple_of(jnp.clip(jb - 1, 0, nblk - 3) * 256, 256)
    nwin = NA_WIN_ROWS * GRID_W
    for h in range(N_HEADS):
        cols = slice(128 * h, 128 * (h + 1))
        q = q_ref[0, :, cols]
        s_w = _dot_nt(q, k_ref[0, pl.ds(start, nwin), cols]) + bias_ref[h, 0]
        state = _online_update(None, s_w, v_ref[0, pl.ds(start, nwin), cols])
        state = _online_update(state, _dot_nt(q, kc_ref[0, :, cols]), vc_ref[0, :, cols])
        o_ref[0, :, cols] = _softmax_finish(state).astype(BF16)


def _na_attention(pa, pa_ctx, bias):
    b, s, _ = pa.shape
    c = pa_ctx.shape[1]
    rows = s // GRID_W
    nblk = rows // NA_ROWS_PER_BLOCK
    tq = NA_ROWS_PER_BLOCK * GRID_W
    nwin = NA_WIN_ROWS * GRID_W
    hw = N_HEADS * 128

    def bias_map(i, j):
        ty = (j > 0).astype(jnp.int32) + (j == nblk - 1).astype(jnp.int32)
        return (0, ty, 0, 0)

    return pl.pallas_call(
        functools.partial(_na_kernel, nblk=nblk),
        grid=(b, nblk),
        in_specs=[
            pl.BlockSpec((1, tq, hw), lambda i, j: (i, j, 0)),
            pl.BlockSpec((1, s, hw), lambda i, j: (i, 0, 1)),
            pl.BlockSpec((1, s, hw), lambda i, j: (i, 0, 2)),
            pl.BlockSpec((1, c, hw), lambda i, j: (i, 0, 1)),
            pl.BlockSpec((1, c, hw), lambda i, j: (i, 0, 2)),
            pl.BlockSpec((N_HEADS, 1, tq, nwin), bias_map),
        ],
        out_specs=pl.BlockSpec((1, tq, hw), lambda i, j: (i, j, 0)),
        out_shape=jax.ShapeDtypeStruct((b, s, hw), BF16),
        compiler_params=_params("parallel", "arbitrary"),
        name="na_attention",
    )(pa, pa, pa, pa_ctx, pa_ctx, bias)


def _key_chunks(krefs, vrefs):
    chunks = []
    for k_ref, v_ref in zip(krefs, vrefs):
        n = k_ref.shape[1]
        chunks += [(k_ref, v_ref, st, min(BK, n - st)) for st in range(0, n, BK)]
    return chunks


def _head_cols(ref, h, width):
    return slice(0, width) if ref.shape[2] == width else slice(h * width, (h + 1) * width)


def _attn_kernel(q_ref, *refs, dk):
    o_ref = refs[-1]
    chunks = _key_chunks(refs[0:-1:2], refs[1:-1:2])
    states = [None] * HEADS_PER_STEP
    for k_ref, v_ref, st, sz in chunks:
        for h in range(HEADS_PER_STEP):
            q = q_ref[0, :, h * dk:(h + 1) * dk]
            k = k_ref[0, st:st + sz, _head_cols(k_ref, h, dk)]
            v = v_ref[0, st:st + sz, _head_cols(v_ref, h, 128)]
            states[h] = _online_update(states[h], _dot_nt(q, k), v)
    for h in range(HEADS_PER_STEP):
        o_ref[0, :, 128 * h:128 * (h + 1)] = _softmax_finish(states[h]).astype(BF16)


def _diff_kernel(q_ref, *refs, lam_init):
    lq1_ref, lk1_ref, lq2_ref, lk2_ref, g_ref, o_ref = refs[-6:]
    chunks = _key_chunks(refs[0:-6:2], refs[1:-6:2])
    lam = (jnp.exp(jnp.sum(lq1_ref[...] * lk1_ref[...], axis=-1, keepdims=True))
           - jnp.exp(jnp.sum(lq2_ref[...] * lk2_ref[...], axis=-1, keepdims=True)) + lam_init)
    lane = lax.broadcasted_iota(jnp.int32, (1, 128), 1)
    low = (lane < DIFF_D).astype(BF16)
    qs = []
    for h in range(HEADS_PER_STEP):
        q = q_ref[0, :, 128 * h:128 * (h + 1)]
        qs += [q * low, q * (1.0 - low).astype(BF16)]
    states = [None] * len(qs)
    for k_ref, v_ref, st, sz in chunks:
        for i, q in enumerate(qs):
            cols = slice(128 * (i // 2), 128 * (i // 2 + 1))
            states[i] = _online_update(states[i], _dot_nt(q, k_ref[0, st:st + sz, cols]),
                                       v_ref[0, st:st + sz, cols])
    for h in range(HEADS_PER_STEP):
        o = _softmax_finish(states[2 * h]) - lam * _softmax_finish(states[2 * h + 1])
        o_ref[0, :, 128 * h:128 * (h + 1)] = (_rms(o, g_ref[...]) * (1.0 - lam_init)).astype(BF16)


def _attention(name, pq, p_lat, p_ctx, *, dk, k_off, v_off, shared_kv=False, diff=None):
    b, t, _ = pq.shape
    tq = min(TQ, t)
    hp = HEADS_PER_STEP
    kw, vw = (dk, 128) if shared_kv else (hp * dk, hp * 128)
    in_specs = [pl.BlockSpec((1, tq, hp * dk), lambda i, g, j: (i, j, g))]
    operands = [pq]
    for p in (p_lat, p_ctx):
        if p is None:
            continue
        n = p.shape[1]
        in_specs.append(pl.BlockSpec((1, n, kw), lambda i, g, j: (i, 0, k_off // kw + g)))
        in_specs.append(pl.BlockSpec((1, n, vw), lambda i, g, j: (i, 0, v_off // vw + g)))
        operands += [p, p]
    if diff is None:
        kernel = functools.partial(_attn_kernel, dk=dk)
    else:
        lam_vecs, g_sub, lam_init = diff
        kernel = functools.partial(_diff_kernel, lam_init=lam_init)
        in_specs += [_const_spec((1, DIFF_D))] * 4 + [_const_spec((1, 128))]
        operands += [v.reshape(1, DIFF_D) for v in lam_vecs] + [g_sub.reshape(1, 128)]
    return pl.pallas_call(
        kernel,
        grid=(b, N_HEADS // hp, t // tq),
        in_specs=in_specs,
        out_specs=pl.BlockSpec((1, tq, hp * 128), lambda i, g, j: (i, j, g)),
        out_shape=jax.ShapeDtypeStruct((b, t, N_HEADS * 128), BF16),
        compiler_params=_params("parallel", "parallel", "arbitrary"),
        name=name,
    )(*operands)


def _out_kernel(oa_ref, ob_ref, oc_ref, od_ref, w_ref, x_ref, gt_ref, g_ref, sc_ref, sh_ref,
                xo_ref, ho_ref, *, ctx_row):
    gate, sc, sh = _mod_row(gt_ref, ctx_row), _mod_row(sc_ref, ctx_row), _mod_row(sh_ref, ctx_row)
    for rows in _row_subtiles(x_ref.shape[1]):
        y = _dot(oa_ref[0, rows, :], w_ref[0, 0:512, :])
        y = y + _dot(ob_ref[0, rows, :], w_ref[0, 512:1024, :])
        y = y + _dot(oc_ref[0, rows, :], w_ref[0, 1024:1536, :])
        y = y + _dot(od_ref[0, rows, :], w_ref[0, 1536:2048, :])
        xn = x_ref[0, rows, :] + gate * y
        xo_ref[0, rows, :] = xn
        ho_ref[0, rows, :] = _modnorm(xn, g_ref[...], sc, sh).astype(BF16)


def _out_proj(outs, w_out, x, g_ffn, mod, layer, ctx_row):
    b, t, d = x.shape
    tm = min(TM_DENSE, t)
    o_spec = pl.BlockSpec((1, tm, 512), lambda i, j: (i, j, 0))
    x_spec = pl.BlockSpec((1, tm, d), lambda i, j: (i, j, 0))
    return pl.pallas_call(
        functools.partial(_out_kernel, ctx_row=ctx_row),
        grid=(b, t // tm),
        in_specs=[o_spec] * 4 + [
            pl.BlockSpec((1,) + w_out.shape[1:], lambda i, j: (layer, 0, 0)),
            x_spec,
            _mod_spec(layer, 2),
            _const_spec((1, d)),
            _mod_spec(layer, 4),
            _mod_spec(layer, 3),
        ],
        out_specs=[x_spec, x_spec],
        out_shape=[jax.ShapeDtypeStruct((b, t, d), F32), jax.ShapeDtypeStruct((b, t, d), BF16)],
        compiler_params=_params("parallel", "parallel"),
        name="out_proj",
    )(*outs, w_out, x, mod, g_ffn.reshape(1, d), mod, mod)


def _ffn_up_kernel(h_ref, wg_ref, wu_ref, o_ref, wgb_ref, wub_ref):
    @pl.when((pl.program_id(1) == 0) & (pl.program_id(2) == 0))
    def _():
        wgb_ref[...] = wg_ref[0].astype(BF16)
        wub_ref[...] = wu_ref[0].astype(BF16)

    for rows in _row_subtiles(h_ref.shape[1]):
        h = h_ref[0, rows, :]
        gate = _dot(h, wgb_ref[...])
        up = _dot(h, wub_ref[...])
        o_ref[0, rows, :] = ((gate / (1.0 + jnp.exp(-gate))) * up).astype(BF16)


def _ffn_up(h, wg, wu, layer):
    b, t, d = h.shape
    dff = wg.shape[2]
    tm = min(TM_UP, t)
    w_spec = pl.BlockSpec((1, d, TF), lambda f, i, j: (layer, 0, f))
    return pl.pallas_call(
        _ffn_up_kernel,
        grid=(dff // TF, b, t // tm),
        in_specs=[pl.BlockSpec((1, tm, d), lambda f, i, j: (i, j, 0)), w_spec, w_spec],
        out_specs=pl.BlockSpec((1, tm, TF), lambda f, i, j: (i, j, f)),
        out_shape=jax.ShapeDtypeStruct((b, t, dff), BF16),
        scratch_shapes=[pltpu.VMEM((d, TF), BF16), pltpu.VMEM((d, TF), BF16)],
        compiler_params=_params("arbitrary", "arbitrary", "arbitrary"),
        name="ffn_up",
    )(h, wg, wu)


def _ffn_down_kernel(a_ref, wd_ref, x_ref, gt_ref, g_ref, *rest, ctx_row, final):
    xn = x_ref[0] + _mod_row(gt_ref, ctx_row) * _dot(a_ref[0], wd_ref[0])
    if final:
        (o_ref,) = rest
        o_ref[0] = _rms(xn, g_ref[...])
    else:
        sc_ref, sh_ref, xo_ref, ho_ref = rest
        xo_ref[0] = xn
        ho_ref[0] = _modnorm(xn, g_ref[...], _mod_row(sc_ref, ctx_row),
                             _mod_row(sh_ref, ctx_row)).astype(BF16)


def _ffn_down(act, wd, x, g_next, mod, layer, ctx_row, final):
    b, t, d = x.shape
    dff = wd.shape[1]
    tm = min(TM_DOWN, t)
    row_spec = pl.BlockSpec((1, tm, d), lambda i, j: (i, j, 0))
    in_specs = [
        pl.BlockSpec((1, tm, dff), lambda i, j: (i, j, 0)),
        pl.BlockSpec((1, dff, d), lambda i, j: (layer, 0, 0), pipeline_mode=pl.Buffered(1)),
        row_spec,
        _mod_spec(layer, 5),
        _const_spec((1, d)),
    ]
    operands = [act, wd, x, mod, g_next.reshape(1, d)]
    if final:
        out_specs = row_spec
        out_shape = jax.ShapeDtypeStruct((b, t, d), F32)
    else:
        in_specs += [_mod_spec(layer + 1, 1), _mod_spec(layer + 1, 0)]
        operands += [mod, mod]
        out_specs = [row_spec, row_spec]
        out_shape = [jax.ShapeDtypeStruct((b, t, d), F32), jax.ShapeDtypeStruct((b, t, d), BF16)]
    return pl.pallas_call(
        functools.partial(_ffn_down_kernel, ctx_row=ctx_row, final=final),
        grid=(b, t // tm),
        in_specs=in_specs,
        out_specs=out_specs,
        out_shape=out_shape,
        compiler_params=_params("parallel", "parallel"),
        name="ffn_down",
    )(*operands)


def _ffn(h, wg, wu, wd, x, g_next, mod, layer, ctx_row, final):
    return _ffn_down(_ffn_up(h, wg, wu, layer), wd, x, g_next, mod, layer, ctx_row, final)


def _rope_tables(s, n_ctx):
    t = jnp.arange(s, dtype=jnp.int32)
    row = (t // GRID_W).astype(F32)
    col = (t % GRID_W).astype(F32)

    def cos_sin(dim):
        d_axis = dim // 2
        inv = 1.0 / (ROPE_THETA ** (jnp.arange(0, d_axis, 2, dtype=F32) / d_axis))
        ang = jnp.concatenate([row[:, None] * inv, col[:, None] * inv], axis=-1)
        return jnp.cos(ang), jnp.sin(ang)

    c128, s128 = cos_sin(HEAD_DIM)
    c64, s64 = cos_sin(MLA_ROPE)
    z = jnp.zeros_like(s64)
    lat = (
        jnp.concatenate([c128, c128], axis=-1),
        jnp.concatenate([-s128, s128], axis=-1),
        jnp.tile(c64, (1, 4)),
        jnp.tile(jnp.concatenate([-s64, z], axis=-1), (1, 2)),
        jnp.tile(jnp.concatenate([z, s64], axis=-1), (1, 2)),
    )
    one = jnp.ones((n_ctx, 128), F32)
    zero = jnp.zeros((n_ctx, 128), F32)
    return lat, (one, zero, one, zero, zero)


def kernel(x, c, ctx, c_ctx, w_ada, b_ada, g_attn, g_ffn, w_in, w_out, na_rpb, gqa_gq, gqa_gk,
           mla_gq, mla_gkv, mla_wuq, mla_wukv, diff_lq1, diff_lk1, diff_lq2, diff_lk2, diff_gsub,
           ffn_wg, ffn_wu, ffn_wd, g_final):
    depth = w_ada.shape[0]
    b, s, d = x.shape
    n_ctx = ctx.shape[1]
    rows = s // GRID_W
    ctx_row = b

    mod = _modulation(c, c_ctx, w_ada, b_ada)
    w_pad = jnp.zeros(w_in.shape[:2] + (128 - MLA_ROPE,), w_in.dtype)
    w_all = jnp.concatenate([w_in[:, :, :OFF_D], w_pad, w_in[:, :, OFF_D:]], axis=2).astype(BF16)
    wo, wd = w_out.astype(BF16), ffn_wd.astype(BF16)
    wg, wu = ffn_wg, ffn_wu
    tabs_lat, tabs_ctx = _rope_tables(s, b * n_ctx)

    def per_sample(a):
        return a.reshape(b, n_ctx, a.shape[-1])

    def flat(a):
        return a.reshape(1, b * n_ctx, a.shape[-1])

    xl, xc = x, flat(ctx)
    hl, hc = xl, xc

    for l in range(depth):
        ctx_out = l < depth - 1
        lam_init = 0.8 - 0.6 * math.exp(-0.3 * l)
        wuq = jnp.pad(mla_wuq[l].astype(BF16).reshape(MLA_Q_RANK, N_HEADS, MLA_NOPE + MLA_ROPE),
                      ((0, 0), (0, 0), (0, 256 - MLA_NOPE - MLA_ROPE))).reshape(MLA_Q_RANK, N_HEADS * 256)
        wukv = mla_wukv[l].astype(BF16).reshape(MLA_KV_RANK, N_HEADS, 2, 128)
        wukv = wukv.transpose(0, 2, 1, 3).reshape(MLA_KV_RANK, 2 * N_HEADS * 128)
        gq, gk = gqa_gq[l].reshape(1, 128), gqa_gk[l].reshape(1, 128)
        mgq, mgkv = mla_gq[l].reshape(1, MLA_Q_RANK), mla_gkv[l].reshape(1, MLA_KV_RANK)
        bias = _na_bias(na_rpb[l], rows)
        diff = ((diff_lq1[l], diff_lk1[l], diff_lq2[l], diff_lk2[l]), diff_gsub[l], lam_init)

        small = [gq, gk, mgq, mgkv, wuq, wukv]
        norm = (g_attn[0], mod) if l == 0 else None
        pa, pb, pc, pd = _in_proj(hl, w_all, l, small, tabs_lat, norm)
        pa_c, pb_c, pc_c, pd_c = (per_sample(p) for p in _in_proj(hc, w_all, l, small, tabs_ctx, norm, ctx_row))

        gqa_idx = dict(dk=128, k_off=512, v_off=768, shared_kv=True)
        mla_idx = dict(dk=256, k_off=1024, v_off=2048)
        std_idx = dict(dk=128, k_off=512, v_off=1024)

        outs = (
            _na_attention(pa, pa_c, bias),
            _attention("gqa_attention", pb, pb, pb_c, **gqa_idx),
            _attention("mla_attention", pc, pc, pc_c, **mla_idx),
            _attention("diff_attention", pd, pd, pd_c, diff=diff, **std_idx),
        )
        xl, hl = _out_proj(outs, wo, xl, g_ffn[l], mod, l, None)
        if ctx_out:
            outs_c = (
                _attention("na_ctx_attention", pa_c, None, pa_c, **std_idx),
                _attention("gqa_ctx_attention", pb_c, None, pb_c, **gqa_idx),
                _attention("mla_ctx_attention", pc_c, None, pc_c, **mla_idx),
                _attention("diff_ctx_attention", pd_c, None, pd_c, diff=diff, **std_idx),
            )
            xc, hc = _out_proj([flat(o) for o in outs_c], wo, xc, g_ffn[l], mod, l, ctx_row)
            xl, hl = _ffn(hl, wg, wu, wd, xl, g_attn[l + 1], mod, l, None, False)
            xc, hc = _ffn(hc, wg, wu, wd, xc, g_attn[l + 1], mod, l, ctx_row, False)
        else:
            xl = _ffn(hl, wg, wu, wd, xl, g_final, mod, l, None, True)
    return xl
```

```python
import functools
import math

import numpy as np
import jax
import jax.numpy as jnp
from jax import lax
from jax.experimental import pallas as pl
from jax.experimental.pallas import tpu as pltpu

F32 = jnp.float32
BF16 = jnp.bfloat16

D_MODEL = 2048
GRID_W = 64
HEAD_DIM = 128
N_HEADS = 4
ROPE_THETA = 10000.0
EPS = 1e-6
NA_KH = 8
NA_KW = 16
NA_ROWS_PER_BLOCK = 4
NA_WIN_ROWS = 12
GQA_KV_HEADS = 2
MLA_Q_RANK = 512
MLA_KV_RANK = 256
MLA_NOPE = 128
MLA_ROPE = 64
DIFF_D = 64
D_FF = 5632

LOG2E = math.log2(math.e)
NA_SCALE = HEAD_DIM ** -0.5 * LOG2E
GQA_SCALE = HEAD_DIM ** -0.5 * LOG2E
MLA_SCALE = (MLA_NOPE + MLA_ROPE) ** -0.5 * LOG2E
DIFF_SCALE = DIFF_D ** -0.5 * LOG2E
NEG_BIG = -1e30

OFF_D = 3392

V7X_VMEM_LIMIT = 56 * 1024 * 1024

TQ = 512
HEADS_PER_STEP = 2
BK = 2048
TM_DENSE = 512
TM_UP = 2048
TM_DOWN = 256
TF = 512
ROW_SUB = 256


def _row_subtiles(tm):
    rs = min(ROW_SUB, tm)
    return [slice(r, r + rs) for r in range(0, tm, rs)]


def _params(*sem):
    return pltpu.CompilerParams(dimension_semantics=sem, vmem_limit_bytes=V7X_VMEM_LIMIT)


def _dot(a, b):
    return jnp.dot(a, b, preferred_element_type=F32)


def _dot_nt(a, b):
    return lax.dot_general(a, b, (((1,), (1,)), ((), ())), preferred_element_type=F32)


def _mod_row(ref, ctx_row):
    r = pl.program_id(0) if ctx_row is None else ctx_row
    return ref[0, 0, pl.ds(r, 1), :]


def _rms(x, g):
    return x * lax.rsqrt(jnp.mean(x * x, axis=-1, keepdims=True) + EPS) * g


def _modnorm(x, g, sc, sh):
    return _rms(x, g) * (1.0 + sc) + sh


def _rope128(x, cos, sin):
    return x * cos + pltpu.roll(x, 64, 1) * sin


def _rope64(x, c, sa, sb):
    return x * c + pltpu.roll(x, 96, 1) * sa + pltpu.roll(x, 32, 1) * sb


def _mod_kernel(cc_ref, w_ref, b_ref, o_ref):
    cc = cc_ref[...]
    s = cc / (1.0 + jnp.exp(-cc))
    o_ref[0, 0] = _dot(s.astype(BF16), w_ref[0].astype(BF16)) + b_ref[0]


def _modulation(c, c_ctx, w_ada, b_ada):
    depth, d, n = w_ada.shape
    b = c.shape[0]
    assert b < 8
    cc = jnp.zeros((8, d), F32).at[:b].set(c).at[b].set(c_ctx)
    tn = 1024
    per = d // tn
    return pl.pallas_call(
        _mod_kernel,
        grid=(depth, n // tn),
        in_specs=[
            pl.BlockSpec((8, d), lambda l, j: (0, 0)),
            pl.BlockSpec((1, d, tn), lambda l, j: (l, 0, j)),
            pl.BlockSpec((1, 1, tn), lambda l, j: (l, 0, j)),
        ],
        out_specs=pl.BlockSpec((1, 1, 8, tn), lambda l, j: (l, j // per, 0, j % per)),
        out_shape=jax.ShapeDtypeStruct((depth, n // d, 8, d), F32),
        compiler_params=_params("parallel", "parallel"),
        name="adaln_mod",
    )(cc, w_ada, b_ada.reshape(depth, 1, n))


def _mod_spec(layer, chunk):
    return pl.BlockSpec((1, 1, 8, D_MODEL), lambda *_: (layer, chunk, 0, 0))


W_COLS = (0, 1536, 2560, 3456, 4992)
IN_PROJ_WIDTHS = (1536, 1024, 2560, 1536)


def _epi_na(y, rows, o_ref):
    o_ref[0, rows, :512] = (y[:, :512] * NA_SCALE).astype(BF16)
    o_ref[0, rows, 512:] = y[:, 512:].astype(BF16)


def _epi_gqa(y, rows, gq_ref, gk_ref, cos, sin, o_ref):
    for j in range(N_HEADS + GQA_KV_HEADS):
        g = gq_ref[...] if j < N_HEADS else gk_ref[...]
        r = _rope128(_rms(y[:, 128 * j:128 * (j + 1)], g), cos, sin)
        if j < N_HEADS:
            r = r * GQA_SCALE
        o_ref[0, rows, 128 * j:128 * (j + 1)] = r.astype(BF16)
    o_ref[0, rows, 768:] = y[:, 768:].astype(BF16)


def _epi_mla(y, rows, gq_ref, gkv_ref, wuq_ref, wukv_ref, t64, o_ref):
    q = _dot(_rms(y[:, :512], gq_ref[...]).astype(BF16), wuq_ref[...])
    kv = _dot(_rms(y[:, 512:768], gkv_ref[...]).astype(BF16), wukv_ref[...])
    kpe = _rope64(y[:, 768:896], *t64).astype(BF16)
    for h in range(N_HEADS):
        o_ref[0, rows, 256 * h:256 * h + 128] = (q[:, 256 * h:256 * h + 128] * MLA_SCALE).astype(BF16)
        qpe = _rope64(q[:, 256 * h + 128:256 * h + 256], *t64)
        o_ref[0, rows, 256 * h + 128:256 * h + 256] = (qpe * MLA_SCALE).astype(BF16)
        o_ref[0, rows, 1024 + 256 * h:1024 + 256 * h + 128] = kv[:, 128 * h:128 * (h + 1)].astype(BF16)
        o_ref[0, rows, 1024 + 256 * h + 128:1024 + 256 * h + 256] = kpe
    o_ref[0, rows, 2048:] = kv[:, 512:].astype(BF16)


def _epi_diff(y, rows, t64, o_ref):
    for j in range(2 * N_HEADS):
        r = _rope64(y[:, 128 * j:128 * (j + 1)], *t64)
        if j < N_HEADS:
            r = r * DIFF_SCALE
        o_ref[0, rows, 128 * j:128 * (j + 1)] = r.astype(BF16)
    o_ref[0, rows, 1024:] = y[:, 1024:].astype(BF16)


def _in_proj_kernel(*refs, norm, ctx_row):
    if norm:
        h_ref, g_ref, sc_ref, sh_ref, *refs = refs
        g, sc, sh = g_ref[...], _mod_row(sc_ref, ctx_row), _mod_row(sh_ref, ctx_row)
    else:
        h_ref, *refs = refs
    (w_ref, gq_ref, gk_ref, mgq_ref, mgkv_ref, wuq_ref, wukv_ref,
     cos_ref, sin_ref, c_ref, sa_ref, sb_ref, oa_ref, ob_ref, oc_ref, od_ref) = refs
    a0, b0, c0, d0, end = W_COLS
    for rows in _row_subtiles(h_ref.shape[1]):
        h = h_ref[0, rows, :]
        if norm:
            h = _modnorm(h, g, sc, sh).astype(BF16)
        t64 = (c_ref[rows, :], sa_ref[rows, :], sb_ref[rows, :])
        _epi_na(_dot(h, w_ref[0, :, a0:b0]), rows, oa_ref)
        _epi_gqa(_dot(h, w_ref[0, :, b0:c0]), rows, gq_ref, gk_ref, cos_ref[rows, :], sin_ref[rows, :], ob_ref)
        _epi_mla(_dot(h, w_ref[0, :, c0:d0]), rows, mgq_ref, mgkv_ref, wuq_ref, wukv_ref, t64, oc_ref)
        _epi_diff(_dot(h, w_ref[0, :, d0:end]), rows, t64, od_ref)


def _const_spec(shape):
    return pl.BlockSpec(shape, lambda *_: (0,) * len(shape))


def _in_proj(h, w_all, layer, small, tables, norm=None, ctx_row=None):
    b, t, d = h.shape
    tm = min(TM_DENSE, t)
    w_spec = pl.BlockSpec((1,) + w_all.shape[1:], lambda i, j: (layer, 0, 0), pipeline_mode=pl.Buffered(1))
    in_specs = [pl.BlockSpec((1, tm, d), lambda i, j: (i, j, 0))]
    operands = [h]
    if norm is not None:
        g, mod = norm
        in_specs += [_const_spec((1, d)), _mod_spec(layer, 1), _mod_spec(layer, 0)]
        operands += [g.reshape(1, d), mod, mod]
    in_specs += [w_spec] + [_const_spec(s.shape) for s in small]
    in_specs += [pl.BlockSpec((tm, 128), lambda i, j: (j, 0)) for _ in tables]
    return pl.pallas_call(
        functools.partial(_in_proj_kernel, norm=norm is not None, ctx_row=ctx_row),
        grid=(b, t // tm),
        in_specs=in_specs,
        out_specs=[pl.BlockSpec((1, tm, n), lambda i, j: (i, j, 0)) for n in IN_PROJ_WIDTHS],
        out_shape=[jax.ShapeDtypeStruct((b, t, n), BF16) for n in IN_PROJ_WIDTHS],
        compiler_params=_params("parallel", "parallel"),
        name="in_proj",
    )(*operands, w_all, *small, *tables)


def _na_index_tables(rows):
    nblk = rows // NA_ROWS_PER_BLOCK
    dr = np.zeros((3, NA_ROWS_PER_BLOCK, NA_WIN_ROWS), np.int32)
    rv = np.zeros((3, NA_ROWS_PER_BLOCK, NA_WIN_ROWS), bool)
    for ty, jb in enumerate((0, 1, nblk - 1)):
        ws = NA_ROWS_PER_BLOCK * min(max(jb - 1, 0), nblk - 3)
        for a in range(NA_ROWS_PER_BLOCK):
            r = NA_ROWS_PER_BLOCK * jb + a
            r_start = min(max(r - NA_KH // 2, 0), rows - NA_KH)
            for kap in range(NA_WIN_ROWS):
                kr = ws + kap
                ok = r_start <= kr < r_start + NA_KH
                rv[ty, a, kap] = ok
                dr[ty, a, kap] = (kr - r + NA_KH - 1) if ok else 0
    cols = np.arange(GRID_W)
    c_start = np.clip(cols - NA_KW // 2, 0, GRID_W - NA_KW)
    kc = np.arange(GRID_W)[None, :]
    cv = (kc >= c_start[:, None]) & (kc < c_start[:, None] + NA_KW)
    dc = np.where(cv, kc - cols[:, None] + NA_KW - 1, 0).astype(np.int32)
    return dr, rv, dc, cv


def _na_bias(rpb, rows):
    dr, rv, dc, cv = _na_index_tables(rows)
    nh, _, ndc = rpb.shape
    onehot = np.zeros((ndc + 1, GRID_W, GRID_W), np.float32)
    onehot[np.where(cv, dc, ndc), np.arange(GRID_W)[:, None], np.arange(GRID_W)[None, :]] = 1.0
    r = jnp.take(rpb, jnp.asarray(dr.reshape(-1)), axis=1) * LOG2E
    r = r.reshape(nh, 3, NA_ROWS_PER_BLOCK, NA_WIN_ROWS, ndc)
    r = jnp.where(jnp.asarray(rv)[None, ..., None], r, NEG_BIG)
    r = jnp.concatenate([r, jnp.full(r.shape[:-1] + (1,), NEG_BIG, F32)], axis=-1)
    full = jnp.einsum("htakd,dcj->htackj", r, jnp.asarray(onehot), precision=lax.Precision.HIGHEST)
    return full.reshape(nh, 3, NA_ROWS_PER_BLOCK * GRID_W, NA_WIN_ROWS * GRID_W)


def _lane_fold(x, op):
    acc = x[:, :128]
    for j in range(1, x.shape[1] // 128):
        acc = op(acc, x[:, 128 * j:128 * (j + 1)])
    return acc


def _online_update(state, s, v):
    v1 = jnp.concatenate([v, jnp.ones_like(v)], axis=1)
    mc = jnp.max(_lane_fold(s, jnp.maximum), axis=-1, keepdims=True)
    if state is None:
        return mc, _dot(jnp.exp2(s - mc).astype(BF16), v1)
    m, acc = state
    m_new = jnp.maximum(m, mc)
    p = jnp.exp2(s - m_new)
    return m_new, acc * jnp.exp2(m - m_new) + _dot(p.astype(BF16), v1)


def _softmax_finish(state):
    _, acc = state
    return acc[:, :128] / acc[:, 128:]


def _na_kernel(q_ref, k_ref, v_ref, kc_ref, vc_ref, bias_ref, o_ref, *, nblk):
    jb = pl.program_id(1)
    start = pl.multiple_of(jnp.clip(jb - 1, 0, nblk - 3) * 256, 256)
    nwin = NA_WIN_ROWS * GRID_W
    def scores(h):
        cols = slice(128 * h, 128 * (h + 1))
        q = q_ref[0, :, cols]
        return _dot_nt(q, k_ref[0, pl.ds(start, nwin), cols]) + bias_ref[h, 0], _dot_nt(q, kc_ref[0, :, cols])

    s_next = scores(0)
    for h in range(N_HEADS):
        (s_w, s_c), s_next = s_next, (scores(h + 1) if h + 1 < N_HEADS else None)
        cols = slice(128 * h, 128 * (h + 1))
        state = _online_update(None, s_w, v_ref[0, pl.ds(start, nwin), cols])
        state = _online_update(state, s_c, vc_ref[0, :, cols])
        o_ref[0, :, cols] = _softmax_finish(state).astype(BF16)


def _na_attention(pa, pa_ctx, bias):
    b, s, _ = pa.shape
    c = pa_ctx.shape[1]
    rows = s // GRID_W
    nblk = rows // NA_ROWS_PER_BLOCK
    tq = NA_ROWS_PER_BLOCK * GRID_W
    nwin = NA_WIN_ROWS * GRID_W
    hw = N_HEADS * 128

    def bias_map(i, j):
        ty = (j > 0).astype(jnp.int32) + (j == nblk - 1).astype(jnp.int32)
        return (0, ty, 0, 0)

    return pl.pallas_call(
        functools.partial(_na_kernel, nblk=nblk),
        grid=(b, nblk),
        in_specs=[
            pl.BlockSpec((1, tq, hw), lambda i, j: (i, j, 0)),
            pl.BlockSpec((1, s, hw), lambda i, j: (i, 0, 1)),
            pl.BlockSpec((1, s, hw), lambda i, j: (i, 0, 2)),
            pl.BlockSpec((1, c, hw), lambda i, j: (i, 0, 1)),
            pl.BlockSpec((1, c, hw), lambda i, j: (i, 0, 2)),
            pl.BlockSpec((N_HEADS, 1, tq, nwin), bias_map),
        ],
        out_specs=pl.BlockSpec((1, tq, hw), lambda i, j: (i, j, 0)),
        out_shape=jax.ShapeDtypeStruct((b, s, hw), BF16),
        compiler_params=_params("parallel", "arbitrary"),
        name="na_attention",
    )(pa, pa, pa, pa_ctx, pa_ctx, bias)


def _key_chunks(krefs, vrefs):
    chunks = []
    for k_ref, v_ref in zip(krefs, vrefs):
        n = k_ref.shape[1]
        chunks += [(k_ref, v_ref, st, min(BK, n - st)) for st in range(0, n, BK)]
    return chunks


def _head_cols(ref, h, width):
    return slice(0, width) if ref.shape[2] == width else slice(h * width, (h + 1) * width)


def _chained_softmax(chunks, scores, values):
    states = None
    for c in range(len(chunks)):
        s_cur = scores(c)
        states = states or [None] * len(s_cur)
        for i, s in enumerate(s_cur):
            states[i] = _online_update(states[i], s, values(c, i))
    return states


def _attn_kernel(q_ref, *refs, dk):
    o_ref = refs[-1]
    chunks = _key_chunks(refs[0:-1:2], refs[1:-1:2])

    def scores(c):
        k_ref, _, st, sz = chunks[c]
        return [_dot_nt(q_ref[0, :, h * dk:(h + 1) * dk], k_ref[0, st:st + sz, _head_cols(k_ref, h, dk)])
                for h in range(HEADS_PER_STEP)]

    def values(c, h):
        _, v_ref, st, sz = chunks[c]
        return v_ref[0, st:st + sz, _head_cols(v_ref, h, 128)]

    states = _chained_softmax(chunks, scores, values)
    for h in range(HEADS_PER_STEP):
        o_ref[0, :, 128 * h:128 * (h + 1)] = _softmax_finish(states[h]).astype(BF16)


def _diff_kernel(q_ref, *refs, lam_init):
    lq1_ref, lk1_ref, lq2_ref, lk2_ref, g_ref, o_ref = refs[-6:]
    chunks = _key_chunks(refs[0:-6:2], refs[1:-6:2])
    lam = (jnp.exp(jnp.sum(lq1_ref[...] * lk1_ref[...], axis=-1, keepdims=True))
           - jnp.exp(jnp.sum(lq2_ref[...] * lk2_ref[...], axis=-1, keepdims=True)) + lam_init)
    lane = lax.broadcasted_iota(jnp.int32, (1, 128), 1)
    low = (lane < DIFF_D).astype(BF16)
    qs = []
    for h in range(HEADS_PER_STEP):
        q = q_ref[0, :, 128 * h:128 * (h + 1)]
        qs += [q * low, q * (1.0 - low).astype(BF16)]

    def scores(c):
        k_ref, _, st, sz = chunks[c]
        return [_dot_nt(q, k_ref[0, st:st + sz, 128 * (i // 2):128 * (i // 2 + 1)]) for i, q in enumerate(qs)]

    def values(c, i):
        _, v_ref, st, sz = chunks[c]
        return v_ref[0, st:st + sz, 128 * (i // 2):128 * (i // 2 + 1)]

    states = _chained_softmax(chunks, scores, values)
    for h in range(HEADS_PER_STEP):
        o = _softmax_finish(states[2 * h]) - lam * _softmax_finish(states[2 * h + 1])
        o_ref[0, :, 128 * h:128 * (h + 1)] = (_rms(o, g_ref[...]) * (1.0 - lam_init)).astype(BF16)


def _attention(name, pq, p_lat, p_ctx, *, dk, k_off, v_off, shared_kv=False, diff=None):
    b, t, _ = pq.shape
    tq = min(TQ, t)
    hp = HEADS_PER_STEP
    kw, vw = (dk, 128) if shared_kv else (hp * dk, hp * 128)
    in_specs = [pl.BlockSpec((1, tq, hp * dk), lambda i, g, j: (i, j, g))]
    operands = [pq]
    for p in (p_lat, p_ctx):
        if p is None:
            continue
        n = p.shape[1]
        in_specs.append(pl.BlockSpec((1, n, kw), lambda i, g, j: (i, 0, k_off // kw + g)))
        in_specs.append(pl.BlockSpec((1, n, vw), lambda i, g, j: (i, 0, v_off // vw + g)))
        operands += [p, p]
    if diff is None:
        kernel = functools.partial(_attn_kernel, dk=dk)
    else:
        lam_vecs, g_sub, lam_init = diff
        kernel = functools.partial(_diff_kernel, lam_init=lam_init)
        in_specs += [_const_spec((1, DIFF_D))] * 4 + [_const_spec((1, 128))]
        operands += [v.reshape(1, DIFF_D) for v in lam_vecs] + [g_sub.reshape(1, 128)]
    return pl.pallas_call(
        kernel,
        grid=(b, N_HEADS // hp, t // tq),
        in_specs=in_specs,
        out_specs=pl.BlockSpec((1, tq, hp * 128), lambda i, g, j: (i, j, g)),
        out_shape=jax.ShapeDtypeStruct((b, t, N_HEADS * 128), BF16),
        compiler_params=_params("parallel", "parallel", "arbitrary"),
        name=name,
    )(*operands)


def _out_kernel(oa_ref, ob_ref, oc_ref, od_ref, w_ref, x_ref, gt_ref, g_ref, sc_ref, sh_ref,
                xo_ref, ho_ref, *, ctx_row):
    gate, sc, sh = _mod_row(gt_ref, ctx_row), _mod_row(sc_ref, ctx_row), _mod_row(sh_ref, ctx_row)
    for rows in _row_subtiles(x_ref.shape[1]):
        y = _dot(oa_ref[0, rows, :], w_ref[0, 0:512, :])
        y = y + _dot(ob_ref[0, rows, :], w_ref[0, 512:1024, :])
        y = y + _dot(oc_ref[0, rows, :], w_ref[0, 1024:1536, :])
        y = y + _dot(od_ref[0, rows, :], w_ref[0, 1536:2048, :])
        xn = x_ref[0, rows, :] + gate * y
        xo_ref[0, rows, :] = xn
        ho_ref[0, rows, :] = _modnorm(xn, g_ref[...], sc, sh).astype(BF16)


def _out_proj(outs, w_out, x, g_ffn, mod, layer, ctx_row):
    b, t, d = x.shape
    tm = min(TM_DENSE, t)
    o_spec = pl.BlockSpec((1, tm, 512), lambda i, j: (i, j, 0))
    x_spec = pl.BlockSpec((1, tm, d), lambda i, j: (i, j, 0))
    return pl.pallas_call(
        functools.partial(_out_kernel, ctx_row=ctx_row),
        grid=(b, t // tm),
        in_specs=[o_spec] * 4 + [
            pl.BlockSpec((1,) + w_out.shape[1:], lambda i, j: (layer, 0, 0)),
            x_spec,
            _mod_spec(layer, 2),
            _const_spec((1, d)),
            _mod_spec(layer, 4),
            _mod_spec(layer, 3),
        ],
        out_specs=[x_spec, x_spec],
        out_shape=[jax.ShapeDtypeStruct((b, t, d), F32), jax.ShapeDtypeStruct((b, t, d), BF16)],
        compiler_params=_params("parallel", "parallel"),
        name="out_proj",
    )(*outs, w_out, x, mod, g_ffn.reshape(1, d), mod, mod)


def _ffn_up_kernel(h_ref, wg_ref, wu_ref, o_ref, wgb_ref, wub_ref):
    @pl.when((pl.program_id(1) == 0) & (pl.program_id(2) == 0))
    def _():
        wgb_ref[...] = wg_ref[0].astype(BF16)
        wub_ref[...] = wu_ref[0].astype(BF16)

    for rows in _row_subtiles(h_ref.shape[1]):
        h = h_ref[0, rows, :]
        gate = _dot(h, wgb_ref[...])
        up = _dot(h, wub_ref[...])
        o_ref[0, rows, :] = ((gate / (1.0 + jnp.exp(-gate))) * up).astype(BF16)


def _ffn_up(h, wg, wu, layer):
    b, t, d = h.shape
    dff = wg.shape[2]
    tm = min(TM_UP, t)
    w_spec = pl.BlockSpec((1, d, TF), lambda f, i, j: (layer, 0, f))
    return pl.pallas_call(
        _ffn_up_kernel,
        grid=(dff // TF, b, t // tm),
        in_specs=[pl.BlockSpec((1, tm, d), lambda f, i, j: (i, j, 0)), w_spec, w_spec],
        out_specs=pl.BlockSpec((1, tm, TF), lambda f, i, j: (i, j, f)),
        out_shape=jax.ShapeDtypeStruct((b, t, dff), BF16),
        scratch_shapes=[pltpu.VMEM((d, TF), BF16), pltpu.VMEM((d, TF), BF16)],
        compiler_params=_params("arbitrary", "arbitrary", "arbitrary"),
        name="ffn_up",
    )(h, wg, wu)


def _ffn_down_kernel(a_ref, wd_ref, x_ref, gt_ref, g_ref, *rest, ctx_row, final):
    xn = x_ref[0] + _mod_row(gt_ref, ctx_row) * _dot(a_ref[0], wd_ref[0])
    if final:
        (o_ref,) = rest
        o_ref[0] = _rms(xn, g_ref[...])
    else:
        sc_ref, sh_ref, xo_ref, ho_ref = rest
        xo_ref[0] = xn
        ho_ref[0] = _modnorm(xn, g_ref[...], _mod_row(sc_ref, ctx_row),
                             _mod_row(sh_ref, ctx_row)).astype(BF16)


def _ffn_down(act, wd, x, g_next, mod, layer, ctx_row, final):
    b, t, d = x.shape
    dff = wd.shape[1]
    tm = min(TM_DOWN, t)
    row_spec = pl.BlockSpec((1, tm, d), lambda i, j: (i, j, 0))
    in_specs = [
        pl.BlockSpec((1, tm, dff), lambda i, j: (i, j, 0)),
        pl.BlockSpec((1, dff, d), lambda i, j: (layer, 0, 0), pipeline_mode=pl.Buffered(1)),
        row_spec,
        _mod_spec(layer, 5),
        _const_spec((1, d)),
    ]
    operands = [act, wd, x, mod, g_next.reshape(1, d)]
    if final:
        out_specs = row_spec
        out_shape = jax.ShapeDtypeStruct((b, t, d), F32)
    else:
        in_specs += [_mod_spec(layer + 1, 1), _mod_spec(layer + 1, 0)]
        operands += [mod, mod]
        out_specs = [row_spec, row_spec]
        out_shape = [jax.ShapeDtypeStruct((b, t, d), F32), jax.ShapeDtypeStruct((b, t, d), BF16)]
    return pl.pallas_call(
        functools.partial(_ffn_down_kernel, ctx_row=ctx_row, final=final),
        grid=(b, t // tm),
        in_specs=in_specs,
        out_specs=out_specs,
        out_shape=out_shape,
        compiler_params=_params("parallel", "parallel"),
        name="ffn_down",
    )(*operands)


def _ffn(h, wg, wu, wd, x, g_next, mod, layer, ctx_row, final):
    return _ffn_down(_ffn_up(h, wg, wu, layer), wd, x, g_next, mod, layer, ctx_row, final)


def _rope_tables(s, n_ctx):
    t = jnp.arange(s, dtype=jnp.int32)
    row = (t // GRID_W).astype(F32)
    col = (t % GRID_W).astype(F32)

    def cos_sin(dim):
        d_axis = dim // 2
        inv = 1.0 / (ROPE_THETA ** (jnp.arange(0, d_axis, 2, dtype=F32) / d_axis))
        ang = jnp.concatenate([row[:, None] * inv, col[:, None] * inv], axis=-1)
        return jnp.cos(ang), jnp.sin(ang)

    c128, s128 = cos_sin(HEAD_DIM)
    c64, s64 = cos_sin(MLA_ROPE)
    z = jnp.zeros_like(s64)
    lat = (
        jnp.concatenate([c128, c128], axis=-1),
        jnp.concatenate([-s128, s128], axis=-1),
        jnp.tile(c64, (1, 4)),
        jnp.tile(jnp.concatenate([-s64, z], axis=-1), (1, 2)),
        jnp.tile(jnp.concatenate([z, s64], axis=-1), (1, 2)),
    )
    one = jnp.ones((n_ctx, 128), F32)
    zero = jnp.zeros((n_ctx, 128), F32)
    return lat, (one, zero, one, zero, zero)


def kernel(x, c, ctx, c_ctx, w_ada, b_ada, g_attn, g_ffn, w_in, w_out, na_rpb, gqa_gq, gqa_gk,
           mla_gq, mla_gkv, mla_wuq, mla_wukv, diff_lq1, diff_lk1, diff_lq2, diff_lk2, diff_gsub,
           ffn_wg, ffn_wu, ffn_wd, g_final):
    depth = w_ada.shape[0]
    b, s, d = x.shape
    n_ctx = ctx.shape[1]
    rows = s // GRID_W
    ctx_row = b

    mod = _modulation(c, c_ctx, w_ada, b_ada)
    w_pad = jnp.zeros(w_in.shape[:2] + (128 - MLA_ROPE,), w_in.dtype)
    w_all = jnp.concatenate([w_in[:, :, :OFF_D], w_pad, w_in[:, :, OFF_D:]], axis=2).astype(BF16)
    wo, wd = w_out.astype(BF16), ffn_wd.astype(BF16)
    wg, wu = ffn_wg, ffn_wu
    tabs_lat, tabs_ctx = _rope_tables(s, b * n_ctx)

    def per_sample(a):
        return a.reshape(b, n_ctx, a.shape[-1])

    def flat(a):
        return a.reshape(1, b * n_ctx, a.shape[-1])

    xl, xc = x, flat(ctx)
    hl, hc = xl, xc

    for l in range(depth):
        ctx_out = l < depth - 1
        lam_init = 0.8 - 0.6 * math.exp(-0.3 * l)
        wuq = jnp.pad(mla_wuq[l].astype(BF16).reshape(MLA_Q_RANK, N_HEADS, MLA_NOPE + MLA_ROPE),
                      ((0, 0), (0, 0), (0, 256 - MLA_NOPE - MLA_ROPE))).reshape(MLA_Q_RANK, N_HEADS * 256)
        wukv = mla_wukv[l].astype(BF16).reshape(MLA_KV_RANK, N_HEADS, 2, 128)
        wukv = wukv.transpose(0, 2, 1, 3).reshape(MLA_KV_RANK, 2 * N_HEADS * 128)
        gq, gk = gqa_gq[l].reshape(1, 128), gqa_gk[l].reshape(1, 128)
        mgq, mgkv = mla_gq[l].reshape(1, MLA_Q_RANK), mla_gkv[l].reshape(1, MLA_KV_RANK)
        bias = _na_bias(na_rpb[l], rows)
        diff = ((diff_lq1[l], diff_lk1[l], diff_lq2[l], diff_lk2[l]), diff_gsub[l], lam_init)

        small = [gq, gk, mgq, mgkv, wuq, wukv]
        norm = (g_attn[0], mod) if l == 0 else None
        pa, pb, pc, pd = _in_proj(hl, w_all, l, small, tabs_lat, norm)
        pa_c, pb_c, pc_c, pd_c = (per_sample(p) for p in _in_proj(hc, w_all, l, small, tabs_ctx, norm, ctx_row))

        gqa_idx = dict(dk=128, k_off=512, v_off=768, shared_kv=True)
        mla_idx = dict(dk=256, k_off=1024, v_off=2048)
        std_idx = dict(dk=128, k_off=512, v_off=1024)

        outs = (
            _na_attention(pa, pa_c, bias),
            _attention("gqa_attention", pb, pb, pb_c, **gqa_idx),
            _attention("mla_attention", pc, pc, pc_c, **mla_idx),
            _attention("diff_attention", pd, pd, pd_c, diff=diff, **std_idx),
        )
        xl, hl = _out_proj(outs, wo, xl, g_ffn[l], mod, l, None)
        if ctx_out:
            outs_c = (
                _attention("na_ctx_attention", pa_c, None, pa_c, **std_idx),
                _attention("gqa_ctx_attention", pb_c, None, pb_c, **gqa_idx),
                _attention("mla_ctx_attention", pc_c, None, pc_c, **mla_idx),
                _attention("diff_ctx_attention", pd_c, None, pd_c, diff=diff, **std_idx),
            )
            xc, hc = _out_proj([flat(o) for o in outs_c], wo, xc, g_ffn[l], mod, l, ctx_row)
            xl, hl = _ffn(hl, wg, wu, wd, xl, g_attn[l + 1], mod, l, None, False)
            xc, hc = _ffn(hc, wg, wu, wd, xc, g_attn[l + 1], mod, l, ctx_row, False)
        else:
            xl = _ffn(hl, wg, wu, wd, xl, g_final, mod, l, None, True)
    return xl
```

```python
import functools
import math

import numpy as np
import jax
import jax.numpy as jnp
from jax import lax
from jax.experimental import pallas as pl
from jax.experimental.pallas import tpu as pltpu

F32 = jnp.float32
BF16 = jnp.bfloat16

D_MODEL = 2048
GRID_W = 64
HEAD_DIM = 128
N_HEADS = 4
ROPE_THETA = 10000.0
EPS = 1e-6
NA_KH = 8
NA_KW = 16
NA_ROWS_PER_BLOCK = 4
NA_WIN_ROWS = 12
GQA_KV_HEADS = 2
MLA_Q_RANK = 512
MLA_KV_RANK = 256
MLA_NOPE = 128
MLA_ROPE = 64
DIFF_D = 64
D_FF = 5632

LOG2E = math.log2(math.e)
NA_SCALE = HEAD_DIM ** -0.5 * LOG2E
GQA_SCALE = HEAD_DIM ** -0.5 * LOG2E
MLA_SCALE = (MLA_NOPE + MLA_ROPE) ** -0.5 * LOG2E
DIFF_SCALE = DIFF_D ** -0.5 * LOG2E
NEG_BIG = -1e30

OFF_D = 3392

V7X_VMEM_LIMIT = 56 * 1024 * 1024

TQ = 512
HEADS_PER_STEP = 2
BK = 2048
TM_DENSE = 512
TM_UP = 2048
TM_DOWN = 256
TF = 512
ROW_SUB = 256


def _row_subtiles(tm):
    rs = min(ROW_SUB, tm)
    return [slice(r, r + rs) for r in range(0, tm, rs)]


def _params(*sem):
    return pltpu.CompilerParams(dimension_semantics=sem, vmem_limit_bytes=V7X_VMEM_LIMIT)


def _dot(a, b):
    return jnp.dot(a, b, preferred_element_type=F32)


def _dot_nt(a, b):
    return lax.dot_general(a, b, (((1,), (1,)), ((), ())), preferred_element_type=F32)


def _mod_row(ref, ctx_row):
    r = pl.program_id(0) if ctx_row is None else ctx_row
    return ref[0, 0, pl.ds(r, 1), :]


def _rms(x, g):
    return x * lax.rsqrt(jnp.mean(x * x, axis=-1, keepdims=True) + EPS) * g


def _modnorm(x, g, sc, sh):
    return _rms(x, g) * (1.0 + sc) + sh


def _rope128(x, cos, sin):
    return x * cos + pltpu.roll(x, 64, 1) * sin


def _rope64(x, c, sa, sb):
    return x * c + pltpu.roll(x, 96, 1) * sa + pltpu.roll(x, 32, 1) * sb


def _mod_kernel(cc_ref, w_ref, b_ref, o_ref):
    cc = cc_ref[...]
    s = cc / (1.0 + jnp.exp(-cc))
    o_ref[0, 0] = _dot(s.astype(BF16), w_ref[0].astype(BF16)) + b_ref[0]


def _modulation(c, c_ctx, w_ada, b_ada):
    depth, d, n = w_ada.shape
    b = c.shape[0]
    assert b < 8
    cc = jnp.zeros((8, d), F32).at[:b].set(c).at[b].set(c_ctx)
    tn = 1024
    per = d // tn
    return pl.pallas_call(
        _mod_kernel,
        grid=(depth, n // tn),
        in_specs=[
            pl.BlockSpec((8, d), lambda l, j: (0, 0)),
            pl.BlockSpec((1, d, tn), lambda l, j: (l, 0, j)),
            pl.BlockSpec((1, 1, tn), lambda l, j: (l, 0, j)),
        ],
        out_specs=pl.BlockSpec((1, 1, 8, tn), lambda l, j: (l, j // per, 0, j % per)),
        out_shape=jax.ShapeDtypeStruct((depth, n // d, 8, d), F32),
        compiler_params=_params("parallel", "parallel"),
        name="adaln_mod",
    )(cc, w_ada, b_ada.reshape(depth, 1, n))


def _mod_spec(layer, chunk):
    return pl.BlockSpec((1, 1, 8, D_MODEL), lambda *_: (layer, chunk, 0, 0))


W_COLS = (0, 1536, 2560, 3456, 4992)
IN_PROJ_WIDTHS = (1536, 1024, 2560, 1536)


def _epi_na(y, rows, o_ref):
    o_ref[0, rows, :512] = (y[:, :512] * NA_SCALE).astype(BF16)
    o_ref[0, rows, 512:] = y[:, 512:].astype(BF16)


def _epi_gqa(y, rows, gq_ref, gk_ref, cos, sin, o_ref):
    for j in range(N_HEADS + GQA_KV_HEADS):
        g = gq_ref[...] if j < N_HEADS else gk_ref[...]
        r = _rope128(_rms(y[:, 128 * j:128 * (j + 1)], g), cos, sin)
        if j < N_HEADS:
            r = r * GQA_SCALE
        o_ref[0, rows, 128 * j:128 * (j + 1)] = r.astype(BF16)
    o_ref[0, rows, 768:] = y[:, 768:].astype(BF16)


def _epi_mla(y, rows, gq_ref, gkv_ref, wuq_ref, wukv_ref, t64, o_ref):
    q = _dot(_rms(y[:, :512], gq_ref[...]).astype(BF16), wuq_ref[...])
    kv = _dot(_rms(y[:, 512:768], gkv_ref[...]).astype(BF16), wukv_ref[...])
    kpe = _rope64(y[:, 768:896], *t64).astype(BF16)
    for h in range(N_HEADS):
        o_ref[0, rows, 256 * h:256 * h + 128] = (q[:, 256 * h:256 * h + 128] * MLA_SCALE).astype(BF16)
        qpe = _rope64(q[:, 256 * h + 128:256 * h + 256], *t64)
        o_ref[0, rows, 256 * h + 128:256 * h + 256] = (qpe * MLA_SCALE).astype(BF16)
        o_ref[0, rows, 1024 + 256 * h:1024 + 256 * h + 128] = kv[:, 128 * h:128 * (h + 1)].astype(BF16)
        o_ref[0, rows, 1024 + 256 * h + 128:1024 + 256 * h + 256] = kpe
    o_ref[0, rows, 2048:] = kv[:, 512:].astype(BF16)


def _epi_diff(y, rows, t64, o_ref):
    for j in range(2 * N_HEADS):
        r = _rope64(y[:, 128 * j:128 * (j + 1)], *t64)
        if j < N_HEADS:
            r = r * DIFF_SCALE
        o_ref[0, rows, 128 * j:128 * (j + 1)] = r.astype(BF16)
    o_ref[0, rows, 1024:] = y[:, 1024:].astype(BF16)


def _in_proj_kernel(*refs, norm, ctx_row):
    if norm:
        h_ref, g_ref, sc_ref, sh_ref, *refs = refs
        g, sc, sh = g_ref[...], _mod_row(sc_ref, ctx_row), _mod_row(sh_ref, ctx_row)
    else:
        h_ref, *refs = refs
    (w_ref, gq_ref, gk_ref, mgq_ref, mgkv_ref, wuq_ref, wukv_ref,
     cos_ref, sin_ref, c_ref, sa_ref, sb_ref, oa_ref, ob_ref, oc_ref, od_ref) = refs
    a0, b0, c0, d0, end = W_COLS
    for rows in _row_subtiles(h_ref.shape[1]):
        h = h_ref[0, rows, :]
        if norm:
            h = _modnorm(h, g, sc, sh).astype(BF16)
        t64 = (c_ref[rows, :], sa_ref[rows, :], sb_ref[rows, :])
        _epi_na(_dot(h, w_ref[0, :, a0:b0]), rows, oa_ref)
        _epi_gqa(_dot(h, w_ref[0, :, b0:c0]), rows, gq_ref, gk_ref, cos_ref[rows, :], sin_ref[rows, :], ob_ref)
        _epi_mla(_dot(h, w_ref[0, :, c0:d0]), rows, mgq_ref, mgkv_ref, wuq_ref, wukv_ref, t64, oc_ref)
        _epi_diff(_dot(h, w_ref[0, :, d0:end]), rows, t64, od_ref)


def _const_spec(shape):
    return pl.BlockSpec(shape, lambda *_: (0,) * len(shape))


def _in_proj(h, w_all, layer, small, tables, norm=None, ctx_row=None):
    b, t, d = h.shape
    tm = min(TM_DENSE, t)
    w_spec = pl.BlockSpec((1,) + w_all.shape[1:], lambda i, j: (layer, 0, 0), pipeline_mode=pl.Buffered(1))
    in_specs = [pl.BlockSpec((1, tm, d), lambda i, j: (i, j, 0))]
    operands = [h]
    if norm is not None:
        g, mod = norm
        in_specs += [_const_spec((1, d)), _mod_spec(layer, 1), _mod_spec(layer, 0)]
        operands += [g.reshape(1, d), mod, mod]
    in_specs += [w_spec] + [_const_spec(s.shape) for s in small]
    in_specs += [pl.BlockSpec((tm, 128), lambda i, j: (j, 0)) for _ in tables]
    return pl.pallas_call(
        functools.partial(_in_proj_kernel, norm=norm is not None, ctx_row=ctx_row),
        grid=(b, t // tm),
        in_specs=in_specs,
        out_specs=[pl.BlockSpec((1, tm, n), lambda i, j: (i, j, 0)) for n in IN_PROJ_WIDTHS],
        out_shape=[jax.ShapeDtypeStruct((b, t, n), BF16) for n in IN_PROJ_WIDTHS],
        compiler_params=_params("parallel", "parallel"),
        name="in_proj",
    )(*operands, w_all, *small, *tables)


def _na_index_tables(rows):
    nblk = rows // NA_ROWS_PER_BLOCK
    dr = np.zeros((3, NA_ROWS_PER_BLOCK, NA_WIN_ROWS), np.int32)
    rv = np.zeros((3, NA_ROWS_PER_BLOCK, NA_WIN_ROWS), bool)
    for ty, jb in enumerate((0, 1, nblk - 1)):
        ws = NA_ROWS_PER_BLOCK * min(max(jb - 1, 0), nblk - 3)
        for a in range(NA_ROWS_PER_BLOCK):
            r = NA_ROWS_PER_BLOCK * jb + a
            r_start = min(max(r - NA_KH // 2, 0), rows - NA_KH)
            for kap in range(NA_WIN_ROWS):
                kr = ws + kap
                ok = r_start <= kr < r_start + NA_KH
                rv[ty, a, kap] = ok
                dr[ty, a, kap] = (kr - r + NA_KH - 1) if ok else 0
    cols = np.arange(GRID_W)
    c_start = np.clip(cols - NA_KW // 2, 0, GRID_W - NA_KW)
    kc = np.arange(GRID_W)[None, :]
    cv = (kc >= c_start[:, None]) & (kc < c_start[:, None] + NA_KW)
    dc = np.where(cv, kc - cols[:, None] + NA_KW - 1, 0).astype(np.int32)
    return dr, rv, dc, cv


def _na_bias(rpb, rows):
    dr, rv, dc, cv = _na_index_tables(rows)
    nh, _, ndc = rpb.shape
    nd = ndc + 1
    onehot = np.zeros((2, nd, GRID_W, 2, GRID_W), np.float32)
    for e in range(2):
        onehot[e, np.where(cv, dc, ndc), np.arange(GRID_W)[:, None], e, np.arange(GRID_W)[None, :]] = 1.0
    onehot = onehot.reshape(2 * nd, GRID_W, 2 * GRID_W)
    r = jnp.take(rpb, jnp.asarray(dr.reshape(-1)), axis=1) * LOG2E
    r = r.reshape(nh, 3, NA_ROWS_PER_BLOCK, NA_WIN_ROWS, ndc)
    r = jnp.where(jnp.asarray(rv)[None, ..., None], r, NEG_BIG)
    r = jnp.concatenate([r, jnp.full(r.shape[:-1] + (1,), NEG_BIG, F32)], axis=-1)
    r = r.reshape(nh, 3, NA_ROWS_PER_BLOCK, NA_WIN_ROWS // 2, 2 * nd)
    full = jnp.einsum("htapk,kcz->htpacz", r, jnp.asarray(onehot), precision=lax.Precision.HIGHEST)
    return full.reshape(nh, 3, NA_WIN_ROWS // 2, NA_ROWS_PER_BLOCK * GRID_W, 2 * GRID_W)


def _lane_fold(x, op):
    acc = x[:, :128]
    for j in range(1, x.shape[1] // 128):
        acc = op(acc, x[:, 128 * j:128 * (j + 1)])
    return acc


def _online_update(state, s, v):
    v1 = jnp.concatenate([v, jnp.ones_like(v)], axis=1)
    mc = jnp.max(_lane_fold(s, jnp.maximum), axis=-1, keepdims=True)
    if state is None:
        return mc, _dot(jnp.exp2(s - mc).astype(BF16), v1)
    m, acc = state
    m_new = jnp.maximum(m, mc)
    p = jnp.exp2(s - m_new)
    return m_new, acc * jnp.exp2(m - m_new) + _dot(p.astype(BF16), v1)


def _softmax_finish(state):
    _, acc = state
    return acc[:, :128] / acc[:, 128:]


def _na_kernel(q_ref, k_ref, v_ref, kc_ref, vc_ref, bias_ref, o_ref, *, nblk):
    jb = pl.program_id(1)
    start = pl.multiple_of(jnp.clip(jb - 1, 0, nblk - 3) * 256, 256)
    nwin = NA_WIN_ROWS * GRID_W
    def scores(h):
        cols = slice(128 * h, 128 * (h + 1))
        q = q_ref[0, :, cols]
        bias = jnp.concatenate([bias_ref[h, 0, p] for p in range(bias_ref.shape[2])], axis=1)
        return _dot_nt(q, k_ref[0, pl.ds(start, nwin), cols]) + bias, _dot_nt(q, kc_ref[0, :, cols])

    s_next = scores(0)
    for h in range(N_HEADS):
        (s_w, s_c), s_next = s_next, (scores(h + 1) if h + 1 < N_HEADS else None)
        cols = slice(128 * h, 128 * (h + 1))
        state = _online_update(None, s_w, v_ref[0, pl.ds(start, nwin), cols])
        state = _online_update(state, s_c, vc_ref[0, :, cols])
        o_ref[0, :, cols] = _softmax_finish(state).astype(BF16)


def _na_attention(pa, pa_ctx, bias):
    b, s, _ = pa.shape
    c = pa_ctx.shape[1]
    rows = s // GRID_W
    nblk = rows // NA_ROWS_PER_BLOCK
    tq = NA_ROWS_PER_BLOCK * GRID_W
    nwin = NA_WIN_ROWS * GRID_W
    hw = N_HEADS * 128

    def bias_map(i, j):
        ty = (j > 0).astype(jnp.int32) + (j == nblk - 1).astype(jnp.int32)
        return (0, ty, 0, 0, 0)

    return pl.pallas_call(
        functools.partial(_na_kernel, nblk=nblk),
        grid=(b, nblk),
        in_specs=[
            pl.BlockSpec((1, tq, hw), lambda i, j: (i, j, 0)),
            pl.BlockSpec((1, s, hw), lambda i, j: (i, 0, 1)),
            pl.BlockSpec((1, s, hw), lambda i, j: (i, 0, 2)),
            pl.BlockSpec((1, c, hw), lambda i, j: (i, 0, 1)),
            pl.BlockSpec((1, c, hw), lambda i, j: (i, 0, 2)),
            pl.BlockSpec((N_HEADS, 1, nwin // 128, tq, 128), bias_map),
        ],
        out_specs=pl.BlockSpec((1, tq, hw), lambda i, j: (i, j, 0)),
        out_shape=jax.ShapeDtypeStruct((b, s, hw), BF16),
        compiler_params=_params("parallel", "arbitrary"),
        name="na_attention",
    )(pa, pa, pa, pa_ctx, pa_ctx, bias)


def _key_chunks(krefs, vrefs):
    chunks = []
    for k_ref, v_ref in zip(krefs, vrefs):
        n = k_ref.shape[1]
        chunks += [(k_ref, v_ref, st, min(BK, n - st)) for st in range(0, n, BK)]
    return chunks


def _head_cols(ref, h, width):
    return slice(0, width) if ref.shape[2] == width else slice(h * width, (h + 1) * width)


def _chained_softmax(chunks, scores, values):
    states = None
    for c in range(len(chunks)):
        s_cur = scores(c)
        states = states or [None] * len(s_cur)
        for i, s in enumerate(s_cur):
            states[i] = _online_update(states[i], s, values(c, i))
    return states


def _attn_kernel(q_ref, *refs, dk):
    o_ref = refs[-1]
    chunks = _key_chunks(refs[0:-1:2], refs[1:-1:2])

    def scores(c):
        k_ref, _, st, sz = chunks[c]
        return [_dot_nt(q_ref[0, :, h * dk:(h + 1) * dk], k_ref[0, st:st + sz, _head_cols(k_ref, h, dk)])
                for h in range(HEADS_PER_STEP)]

    def values(c, h):
        _, v_ref, st, sz = chunks[c]
        return v_ref[0, st:st + sz, _head_cols(v_ref, h, 128)]

    states = _chained_softmax(chunks, scores, values)
    for h in range(HEADS_PER_STEP):
        o_ref[0, :, 128 * h:128 * (h + 1)] = _softmax_finish(states[h]).astype(BF16)


def _diff_kernel(q_ref, *refs, lam_init):
    lq1_ref, lk1_ref, lq2_ref, lk2_ref, g_ref, o_ref = refs[-6:]
    chunks = _key_chunks(refs[0:-6:2], refs[1:-6:2])
    lam = (jnp.exp(jnp.sum(lq1_ref[...] * lk1_ref[...], axis=-1, keepdims=True))
           - jnp.exp(jnp.sum(lq2_ref[...] * lk2_ref[...], axis=-1, keepdims=True)) + lam_init)
    lane = lax.broadcasted_iota(jnp.int32, (1, 128), 1)
    low = (lane < DIFF_D).astype(BF16)
    qs = []
    for h in range(HEADS_PER_STEP):
        q = q_ref[0, :, 128 * h:128 * (h + 1)]
        qs += [q * low, q * (1.0 - low).astype(BF16)]

    def scores(c):
        k_ref, _, st, sz = chunks[c]
        return [_dot_nt(q, k_ref[0, st:st + sz, 128 * (i // 2):128 * (i // 2 + 1)]) for i, q in enumerate(qs)]

    def values(c, i):
        _, v_ref, st, sz = chunks[c]
        return v_ref[0, st:st + sz, 128 * (i // 2):128 * (i // 2 + 1)]

    states = _chained_softmax(chunks, scores, values)
    for h in range(HEADS_PER_STEP):
        o = _softmax_finish(states[2 * h]) - lam * _softmax_finish(states[2 * h + 1])
        o_ref[0, :, 128 * h:128 * (h + 1)] = (_rms(o, g_ref[...]) * (1.0 - lam_init)).astype(BF16)


def _attention(name, pq, p_lat, p_ctx, *, dk, k_off, v_off, shared_kv=False, diff=None):
    b, t, _ = pq.shape
    tq = min(TQ, t)
    hp = HEADS_PER_STEP
    kw, vw = (dk, 128) if shared_kv else (hp * dk, hp * 128)
    in_specs = [pl.BlockSpec((1, tq, hp * dk), lambda i, g, j: (i, j, g))]
    operands = [pq]
    for p in (p_lat, p_ctx):
        if p is None:
            continue
        n = p.shape[1]
        in_specs.append(pl.BlockSpec((1, n, kw), lambda i, g, j: (i, 0, k_off // kw + g)))
        in_specs.append(pl.BlockSpec((1, n, vw), lambda i, g, j: (i, 0, v_off // vw + g)))
        operands += [p, p]
    if diff is None:
        kernel = functools.partial(_attn_kernel, dk=dk)
    else:
        lam_vecs, g_sub, lam_init = diff
        kernel = functools.partial(_diff_kernel, lam_init=lam_init)
        in_specs += [_const_spec((1, DIFF_D))] * 4 + [_const_spec((1, 128))]
        operands += [v.reshape(1, DIFF_D) for v in lam_vecs] + [g_sub.reshape(1, 128)]
    return pl.pallas_call(
        kernel,
        grid=(b, N_HEADS // hp, t // tq),
        in_specs=in_specs,
        out_specs=pl.BlockSpec((1, tq, hp * 128), lambda i, g, j: (i, j, g)),
        out_shape=jax.ShapeDtypeStruct((b, t, N_HEADS * 128), BF16),
        compiler_params=_params("parallel", "parallel", "arbitrary"),
        name=name,
    )(*operands)


def _out_kernel(oa_ref, ob_ref, oc_ref, od_ref, w_ref, x_ref, gt_ref, g_ref, sc_ref, sh_ref,
                xo_ref, ho_ref, *, ctx_row):
    gate, sc, sh = _mod_row(gt_ref, ctx_row), _mod_row(sc_ref, ctx_row), _mod_row(sh_ref, ctx_row)
    for rows in _row_subtiles(x_ref.shape[1]):
        y = _dot(oa_ref[0, rows, :], w_ref[0, 0:512, :])
        y = y + _dot(ob_ref[0, rows, :], w_ref[0, 512:1024, :])
        y = y + _dot(oc_ref[0, rows, :], w_ref[0, 1024:1536, :])
        y = y + _dot(od_ref[0, rows, :], w_ref[0, 1536:2048, :])
        xn = x_ref[0, rows, :] + gate * y
        xo_ref[0, rows, :] = xn
        ho_ref[0, rows, :] = _modnorm(xn, g_ref[...], sc, sh).astype(BF16)


def _out_proj(outs, w_out, x, g_ffn, mod, layer, ctx_row):
    b, t, d = x.shape
    tm = min(TM_DENSE, t)
    o_spec = pl.BlockSpec((1, tm, 512), lambda i, j: (i, j, 0))
    x_spec = pl.BlockSpec((1, tm, d), lambda i, j: (i, j, 0))
    return pl.pallas_call(
        functools.partial(_out_kernel, ctx_row=ctx_row),
        grid=(b, t // tm),
        in_specs=[o_spec] * 4 + [
            pl.BlockSpec((1,) + w_out.shape[1:], lambda i, j: (layer, 0, 0)),
            x_spec,
            _mod_spec(layer, 2),
            _const_spec((1, d)),
            _mod_spec(layer, 4),
            _mod_spec(layer, 3),
        ],
        out_specs=[x_spec, x_spec],
        out_shape=[jax.ShapeDtypeStruct((b, t, d), F32), jax.ShapeDtypeStruct((b, t, d), BF16)],
        compiler_params=_params("parallel", "parallel"),
        name="out_proj",
    )(*outs, w_out, x, mod, g_ffn.reshape(1, d), mod, mod)


def _ffn_up_kernel(h_ref, wg_ref, wu_ref, *rest, cast_wd):
    if cast_wd:
        wdi_ref, o_ref, wdo_ref, wgb_ref, wub_ref = rest
        wdo_ref[...] = wdi_ref[0].astype(BF16)
    else:
        o_ref, wgb_ref, wub_ref = rest

    @pl.when((pl.program_id(1) == 0) & (pl.program_id(2) == 0))
    def _():
        wgb_ref[...] = wg_ref[0].astype(BF16)
        wub_ref[...] = wu_ref[0].astype(BF16)

    for rows in _row_subtiles(h_ref.shape[1]):
        h = h_ref[0, rows, :]
        gate = _dot(h, wgb_ref[...])
        up = _dot(h, wub_ref[...])
        o_ref[0, rows, :] = ((gate / (1.0 + jnp.exp(-gate))) * up).astype(BF16)


def _ffn_up(h, wg, wu, layer, wd=None):
    b, t, d = h.shape
    dff = wg.shape[2]
    tm = min(TM_UP, t)
    nb, nt = b, t // tm
    w_spec = pl.BlockSpec((1, d, TF), lambda f, i, j: (layer, 0, f))
    in_specs = [pl.BlockSpec((1, tm, d), lambda f, i, j: (i, j, 0)), w_spec, w_spec]
    out_specs = [pl.BlockSpec((1, tm, TF), lambda f, i, j: (i, j, f))]
    out_shape = [jax.ShapeDtypeStruct((b, t, dff), BF16)]
    operands = [h, wg, wu]
    if wd is not None:
        slab = dff // (dff // TF * nb * nt)
        in_specs.append(pl.BlockSpec((1, slab, d), lambda f, i, j: (layer, (f * nb + i) * nt + j, 0)))
        out_specs.append(pl.BlockSpec((slab, d), lambda f, i, j: ((f * nb + i) * nt + j, 0)))
        out_shape.append(jax.ShapeDtypeStruct((dff, d), BF16))
        operands.append(wd)
    outs = pl.pallas_call(
        functools.partial(_ffn_up_kernel, cast_wd=wd is not None),
        grid=(dff // TF, nb, nt),
        in_specs=in_specs,
        out_specs=out_specs,
        out_shape=out_shape,
        scratch_shapes=[pltpu.VMEM((d, TF), BF16), pltpu.VMEM((d, TF), BF16)],
        compiler_params=_params("arbitrary", "arbitrary", "arbitrary"),
        name="ffn_up",
    )(*operands)
    return outs if wd is not None else outs[0]


def _ffn_down_kernel(a_ref, wd_ref, x_ref, gt_ref, g_ref, *rest, ctx_row, final):
    xn = x_ref[0] + _mod_row(gt_ref, ctx_row) * _dot(a_ref[0], wd_ref[...])
    if final:
        (o_ref,) = rest
        o_ref[0] = _rms(xn, g_ref[...])
    else:
        sc_ref, sh_ref, xo_ref, ho_ref = rest
        xo_ref[0] = xn
        ho_ref[0] = _modnorm(xn, g_ref[...], _mod_row(sc_ref, ctx_row),
                             _mod_row(sh_ref, ctx_row)).astype(BF16)


def _ffn_down(act, wd, x, g_next, mod, layer, ctx_row, final):
    b, t, d = x.shape
    dff = wd.shape[0]
    tm = min(TM_DOWN, t)
    row_spec = pl.BlockSpec((1, tm, d), lambda i, j: (i, j, 0))
    in_specs = [
        pl.BlockSpec((1, tm, dff), lambda i, j: (i, j, 0)),
        pl.BlockSpec((dff, d), lambda i, j: (0, 0), pipeline_mode=pl.Buffered(1)),
        row_spec,
        _mod_spec(layer, 5),
        _const_spec((1, d)),
    ]
    operands = [act, wd, x, mod, g_next.reshape(1, d)]
    if final:
        out_specs = row_spec
        out_shape = jax.ShapeDtypeStruct((b, t, d), F32)
    else:
        in_specs += [_mod_spec(layer + 1, 1), _mod_spec(layer + 1, 0)]
        operands += [mod, mod]
        out_specs = [row_spec, row_spec]
        out_shape = [jax.ShapeDtypeStruct((b, t, d), F32), jax.ShapeDtypeStruct((b, t, d), BF16)]
    return pl.pallas_call(
        functools.partial(_ffn_down_kernel, ctx_row=ctx_row, final=final),
        grid=(b, t // tm),
        in_specs=in_specs,
        out_specs=out_specs,
        out_shape=out_shape,
        compiler_params=_params("parallel", "parallel"),
        name="ffn_down",
    )(*operands)


def _ffn(h, wg, wu, wd, x, g_next, mod, layer, ctx_row, final):
    if wd.ndim == 3:
        act, wd = _ffn_up(h, wg, wu, layer, wd)
    else:
        act = _ffn_up(h, wg, wu, layer)
    return _ffn_down(act, wd, x, g_next, mod, layer, ctx_row, final), wd


def _rope_tables(s, n_ctx):
    t = jnp.arange(s, dtype=jnp.int32)
    row = (t // GRID_W).astype(F32)
    col = (t % GRID_W).astype(F32)

    def cos_sin(dim):
        d_axis = dim // 2
        inv = 1.0 / (ROPE_THETA ** (jnp.arange(0, d_axis, 2, dtype=F32) / d_axis))
        ang = jnp.concatenate([row[:, None] * inv, col[:, None] * inv], axis=-1)
        return jnp.cos(ang), jnp.sin(ang)

    c128, s128 = cos_sin(HEAD_DIM)
    c64, s64 = cos_sin(MLA_ROPE)
    z = jnp.zeros_like(s64)
    lat = (
        jnp.concatenate([c128, c128], axis=-1),
        jnp.concatenate([-s128, s128], axis=-1),
        jnp.tile(c64, (1, 4)),
        jnp.tile(jnp.concatenate([-s64, z], axis=-1), (1, 2)),
        jnp.tile(jnp.concatenate([z, s64], axis=-1), (1, 2)),
    )
    one = jnp.ones((n_ctx, 128), F32)
    zero = jnp.zeros((n_ctx, 128), F32)
    return lat, (one, zero, one, zero, zero)


def kernel(x, c, ctx, c_ctx, w_ada, b_ada, g_attn, g_ffn, w_in, w_out, na_rpb, gqa_gq, gqa_gk,
           mla_gq, mla_gkv, mla_wuq, mla_wukv, diff_lq1, diff_lk1, diff_lq2, diff_lk2, diff_gsub,
           ffn_wg, ffn_wu, ffn_wd, g_final):
    depth = w_ada.shape[0]
    b, s, d = x.shape
    n_ctx = ctx.shape[1]
    rows = s // GRID_W
    ctx_row = b

    mod = _modulation(c, c_ctx, w_ada, b_ada)
    w_pad = jnp.zeros(w_in.shape[:2] + (128 - MLA_ROPE,), w_in.dtype)
    w_all = jnp.concatenate([w_in[:, :, :OFF_D], w_pad, w_in[:, :, OFF_D:]], axis=2).astype(BF16)
    wo = w_out.astype(BF16)
    wg, wu = ffn_wg, ffn_wu
    tabs_lat, tabs_ctx = _rope_tables(s, b * n_ctx)

    def per_sample(a):
        return a.reshape(b, n_ctx, a.shape[-1])

    def flat(a):
        return a.reshape(1, b * n_ctx, a.shape[-1])

    xl, xc = x, flat(ctx)
    hl, hc = xl, xc

    for l in range(depth):
        ctx_out = l < depth - 1
        lam_init = 0.8 - 0.6 * math.exp(-0.3 * l)
        wuq = jnp.pad(mla_wuq[l].astype(BF16).reshape(MLA_Q_RANK, N_HEADS, MLA_NOPE + MLA_ROPE),
                      ((0, 0), (0, 0), (0, 256 - MLA_NOPE - MLA_ROPE))).reshape(MLA_Q_RANK, N_HEADS * 256)
        wukv = mla_wukv[l].astype(BF16).reshape(MLA_KV_RANK, N_HEADS, 2, 128)
        wukv = wukv.transpose(0, 2, 1, 3).reshape(MLA_KV_RANK, 2 * N_HEADS * 128)
        gq, gk = gqa_gq[l].reshape(1, 128), gqa_gk[l].reshape(1, 128)
        mgq, mgkv = mla_gq[l].reshape(1, MLA_Q_RANK), mla_gkv[l].reshape(1, MLA_KV_RANK)
        bias = _na_bias(na_rpb[l], rows)
        diff = ((diff_lq1[l], diff_lk1[l], diff_lq2[l], diff_lk2[l]), diff_gsub[l], lam_init)

        small = [gq, gk, mgq, mgkv, wuq, wukv]
        norm = (g_attn[0], mod) if l == 0 else None
        pa, pb, pc, pd = _in_proj(hl, w_all, l, small, tabs_lat, norm)
        pa_c, pb_c, pc_c, pd_c = (per_sample(p) for p in _in_proj(hc, w_all, l, small, tabs_ctx, norm, ctx_row))

        gqa_idx = dict(dk=128, k_off=512, v_off=768, shared_kv=True)
        mla_idx = dict(dk=256, k_off=1024, v_off=2048)
        std_idx = dict(dk=128, k_off=512, v_off=1024)

        outs = (
            _na_attention(pa, pa_c, bias),
            _attention("gqa_attention", pb, pb, pb_c, **gqa_idx),
            _attention("mla_attention", pc, pc, pc_c, **mla_idx),
            _attention("diff_attention", pd, pd, pd_c, diff=diff, **std_idx),
        )
        xl, hl = _out_proj(outs, wo, xl, g_ffn[l], mod, l, None)
        if ctx_out:
            outs_c = (
                _attention("na_ctx_attention", pa_c, None, pa_c, **std_idx),
                _attention("gqa_ctx_attention", pb_c, None, pb_c, **gqa_idx),
                _attention("mla_ctx_attention", pc_c, None, pc_c, **mla_idx),
                _attention("diff_ctx_attention", pd_c, None, pd_c, diff=diff, **std_idx),
            )
            xc, hc = _out_proj([flat(o) for o in outs_c], wo, xc, g_ffn[l], mod, l, ctx_row)
            (xl, hl), wd = _ffn(hl, wg, wu, ffn_wd, xl, g_attn[l + 1], mod, l, None, False)
            (xc, hc), _ = _ffn(hc, wg, wu, wd, xc, g_attn[l + 1], mod, l, ctx_row, False)
        else:
            xl, _ = _ffn(hl, wg, wu, ffn_wd, xl, g_final, mod, l, None, True)
    return xl
```

```python
import functools
import math

import numpy as np
import jax
import jax.numpy as jnp
from jax import lax
from jax.experimental import pallas as pl
from jax.experimental.pallas import tpu as pltpu

F32 = jnp.float32
BF16 = jnp.bfloat16

D_MODEL = 2048
GRID_W = 64
HEAD_DIM = 128
N_HEADS = 4
ROPE_THETA = 10000.0
EPS = 1e-6
NA_KH = 8
NA_KW = 16
NA_ROWS_PER_BLOCK = 4
NA_WIN_ROWS = 12
GQA_KV_HEADS = 2
MLA_Q_RANK = 512
MLA_KV_RANK = 256
MLA_NOPE = 128
MLA_ROPE = 64
DIFF_D = 64
D_FF = 5632

LOG2E = math.log2(math.e)
NA_SCALE = HEAD_DIM ** -0.5 * LOG2E
GQA_SCALE = HEAD_DIM ** -0.5 * LOG2E
MLA_SCALE = (MLA_NOPE + MLA_ROPE) ** -0.5 * LOG2E
DIFF_SCALE = DIFF_D ** -0.5 * LOG2E
NEG_BIG = -1e30

OFF_D = 3392

V7X_VMEM_LIMIT = 56 * 1024 * 1024

TQ = 512
HEADS_PER_STEP = 2
BK = 2048
TM_DENSE = 512
TM_UP = 2048
TM_DOWN = 256
TF = 512
ROW_SUB = 256


def _row_subtiles(tm):
    rs = min(ROW_SUB, tm)
    return [slice(r, r + rs) for r in range(0, tm, rs)]


def _params(*sem):
    return pltpu.CompilerParams(dimension_semantics=sem, vmem_limit_bytes=V7X_VMEM_LIMIT)


def _dot(a, b):
    return jnp.dot(a, b, preferred_element_type=F32)


def _dot_nt(a, b):
    return lax.dot_general(a, b, (((1,), (1,)), ((), ())), preferred_element_type=F32)


def _mod_row(ref, ctx_row):
    r = pl.program_id(0) if ctx_row is None else ctx_row
    return ref[0, 0, pl.ds(r, 1), :]


def _rms(x, g):
    return x * lax.rsqrt(jnp.mean(x * x, axis=-1, keepdims=True) + EPS) * g


def _modnorm(x, g, sc, sh):
    return _rms(x, g) * (1.0 + sc) + sh


def _rope128(x, cos, sin):
    return x * cos + pltpu.roll(x, 64, 1) * sin


def _rope64(x, c, sa, sb):
    return x * c + pltpu.roll(x, 96, 1) * sa + pltpu.roll(x, 32, 1) * sb


def _mod_kernel(cc_ref, w_ref, b_ref, o_ref):
    cc = cc_ref[...]
    s = cc / (1.0 + jnp.exp(-cc))
    o_ref[0, 0] = _dot(s.astype(BF16), w_ref[0].astype(BF16)) + b_ref[0]


def _modulation(c, c_ctx, w_ada, b_ada):
    depth, d, n = w_ada.shape
    b = c.shape[0]
    assert b < 8
    cc = jnp.zeros((8, d), F32).at[:b].set(c).at[b].set(c_ctx)
    tn = 1024
    per = d // tn
    return pl.pallas_call(
        _mod_kernel,
        grid=(depth, n // tn),
        in_specs=[
            pl.BlockSpec((8, d), lambda l, j: (0, 0)),
            pl.BlockSpec((1, d, tn), lambda l, j: (l, 0, j)),
            pl.BlockSpec((1, 1, tn), lambda l, j: (l, 0, j)),
        ],
        out_specs=pl.BlockSpec((1, 1, 8, tn), lambda l, j: (l, j // per, 0, j % per)),
        out_shape=jax.ShapeDtypeStruct((depth, n // d, 8, d), F32),
        compiler_params=_params("parallel", "parallel"),
        name="adaln_mod",
    )(cc, w_ada, b_ada.reshape(depth, 1, n))


def _mod_spec(layer, chunk):
    return pl.BlockSpec((1, 1, 8, D_MODEL), lambda *_: (layer, chunk, 0, 0))


W_COLS = (0, 1536, 2560, 3456, 4992)
PACK_ROWS = 256


def _pack_kernel(w_ref, o_ref):
    x = w_ref[0]
    pad = jnp.zeros((x.shape[0], W_COLS[3] - OFF_D), F32)
    o_ref[0] = jnp.concatenate([x[:, :OFF_D], pad, x[:, OFF_D:]], axis=1).astype(BF16)


def _pack_in_proj_weight(w_in):
    depth, d, n = w_in.shape
    n_out = W_COLS[-1]
    return pl.pallas_call(
        _pack_kernel,
        grid=(depth, d // PACK_ROWS),
        in_specs=[pl.BlockSpec((1, PACK_ROWS, n), lambda l, i: (l, i, 0))],
        out_specs=pl.BlockSpec((1, PACK_ROWS, n_out), lambda l, i: (l, i, 0)),
        out_shape=jax.ShapeDtypeStruct((depth, d, n_out), BF16),
        compiler_params=_params("parallel", "parallel"),
        name="pack_in_proj_weight",
    )(w_in)
IN_PROJ_WIDTHS = (1536, 1024, 2560, 1536)


def _epi_na(y, rows, o_ref):
    o_ref[0, rows, :512] = (y[:, :512] * NA_SCALE).astype(BF16)
    o_ref[0, rows, 512:] = y[:, 512:].astype(BF16)


def _epi_gqa(y, rows, gq_ref, gk_ref, cos, sin, o_ref):
    for j in range(N_HEADS + GQA_KV_HEADS):
        g = gq_ref[...] if j < N_HEADS else gk_ref[...]
        r = _rope128(_rms(y[:, 128 * j:128 * (j + 1)], g), cos, sin)
        if j < N_HEADS:
            r = r * GQA_SCALE
        o_ref[0, rows, 128 * j:128 * (j + 1)] = r.astype(BF16)
    o_ref[0, rows, 768:] = y[:, 768:].astype(BF16)


def _epi_mla(y, rows, gq_ref, gkv_ref, wuq_ref, wukv_ref, t64, o_ref):
    q = _dot(_rms(y[:, :512], gq_ref[...]).astype(BF16), wuq_ref[...])
    kv = _dot(_rms(y[:, 512:768], gkv_ref[...]).astype(BF16), wukv_ref[...])
    kpe = _rope64(y[:, 768:896], *t64).astype(BF16)
    for h in range(N_HEADS):
        o_ref[0, rows, 256 * h:256 * h + 128] = (q[:, 256 * h:256 * h + 128] * MLA_SCALE).astype(BF16)
        qpe = _rope64(q[:, 256 * h + 128:256 * h + 256], *t64)
        o_ref[0, rows, 256 * h + 128:256 * h + 256] = (qpe * MLA_SCALE).astype(BF16)
        o_ref[0, rows, 1024 + 256 * h:1024 + 256 * h + 128] = kv[:, 128 * h:128 * (h + 1)].astype(BF16)
        o_ref[0, rows, 1024 + 256 * h + 128:1024 + 256 * h + 256] = kpe
    o_ref[0, rows, 2048:] = kv[:, 512:].astype(BF16)


def _epi_diff(y, rows, t64, o_ref):
    for j in range(2 * N_HEADS):
        r = _rope64(y[:, 128 * j:128 * (j + 1)], *t64)
        if j < N_HEADS:
            r = r * DIFF_SCALE
        o_ref[0, rows, 128 * j:128 * (j + 1)] = r.astype(BF16)
    o_ref[0, rows, 1024:] = y[:, 1024:].astype(BF16)


def _in_proj_kernel(*refs, norm, ctx_row):
    if norm:
        h_ref, g_ref, sc_ref, sh_ref, *refs = refs
        g, sc, sh = g_ref[...], _mod_row(sc_ref, ctx_row), _mod_row(sh_ref, ctx_row)
    else:
        h_ref, *refs = refs
    (w_ref, gq_ref, gk_ref, mgq_ref, mgkv_ref, wuq_ref, wukv_ref,
     cos_ref, sin_ref, c_ref, sa_ref, sb_ref, oa_ref, ob_ref, oc_ref, od_ref) = refs
    a0, b0, c0, d0, end = W_COLS
    for rows in _row_subtiles(h_ref.shape[1]):
        h = h_ref[0, rows, :]
        if norm:
            h = _modnorm(h, g, sc, sh).astype(BF16)
        t64 = (c_ref[rows, :], sa_ref[rows, :], sb_ref[rows, :])
        _epi_na(_dot(h, w_ref[0, :, a0:b0]), rows, oa_ref)
        _epi_gqa(_dot(h, w_ref[0, :, b0:c0]), rows, gq_ref, gk_ref, cos_ref[rows, :], sin_ref[rows, :], ob_ref)
        _epi_mla(_dot(h, w_ref[0, :, c0:d0]), rows, mgq_ref, mgkv_ref, wuq_ref, wukv_ref, t64, oc_ref)
        _epi_diff(_dot(h, w_ref[0, :, d0:end]), rows, t64, od_ref)


def _const_spec(shape):
    return pl.BlockSpec(shape, lambda *_: (0,) * len(shape))


def _in_proj(h, w_all, layer, small, tables, norm=None, ctx_row=None):
    b, t, d = h.shape
    tm = min(TM_DENSE, t)
    w_spec = pl.BlockSpec((1,) + w_all.shape[1:], lambda i, j: (layer, 0, 0), pipeline_mode=pl.Buffered(1))
    in_specs = [pl.BlockSpec((1, tm, d), lambda i, j: (i, j, 0))]
    operands = [h]
    if norm is not None:
        g, mod = norm
        in_specs += [_const_spec((1, d)), _mod_spec(layer, 1), _mod_spec(layer, 0)]
        operands += [g.reshape(1, d), mod, mod]
    in_specs += [w_spec] + [_const_spec(s.shape) for s in small]
    in_specs += [pl.BlockSpec((tm, 128), lambda i, j: (j, 0)) for _ in tables]
    return pl.pallas_call(
        functools.partial(_in_proj_kernel, norm=norm is not None, ctx_row=ctx_row),
        grid=(b, t // tm),
        in_specs=in_specs,
        out_specs=[pl.BlockSpec((1, tm, n), lambda i, j: (i, j, 0)) for n in IN_PROJ_WIDTHS],
        out_shape=[jax.ShapeDtypeStruct((b, t, n), BF16) for n in IN_PROJ_WIDTHS],
        compiler_params=_params("parallel", "parallel"),
        name="in_proj",
    )(*operands, w_all, *small, *tables)


def _na_index_tables(rows):
    nblk = rows // NA_ROWS_PER_BLOCK
    dr = np.zeros((3, NA_ROWS_PER_BLOCK, NA_WIN_ROWS), np.int32)
    rv = np.zeros((3, NA_ROWS_PER_BLOCK, NA_WIN_ROWS), bool)
    for ty, jb in enumerate((0, 1, nblk - 1)):
        ws = NA_ROWS_PER_BLOCK * min(max(jb - 1, 0), nblk - 3)
        for a in range(NA_ROWS_PER_BLOCK):
            r = NA_ROWS_PER_BLOCK * jb + a
            r_start = min(max(r - NA_KH // 2, 0), rows - NA_KH)
            for kap in range(NA_WIN_ROWS):
                kr = ws + kap
                ok = r_start <= kr < r_start + NA_KH
                rv[ty, a, kap] = ok
                dr[ty, a, kap] = (kr - r + NA_KH - 1) if ok else 0
    cols = np.arange(GRID_W)
    c_start = np.clip(cols - NA_KW // 2, 0, GRID_W - NA_KW)
    kc = np.arange(GRID_W)[None, :]
    cv = (kc >= c_start[:, None]) & (kc < c_start[:, None] + NA_KW)
    dc = np.where(cv, kc - cols[:, None] + NA_KW - 1, 0).astype(np.int32)
    return dr, rv, dc, cv


def _na_bias(rpb, rows):
    dr, rv, dc, cv = _na_index_tables(rows)
    nh, _, ndc = rpb.shape
    nd = ndc + 1
    onehot = np.zeros((2, nd, GRID_W, 2, GRID_W), np.float32)
    for e in range(2):
        onehot[e, np.where(cv, dc, ndc), np.arange(GRID_W)[:, None], e, np.arange(GRID_W)[None, :]] = 1.0
    onehot = onehot.reshape(2 * nd, GRID_W, 2 * GRID_W)
    r = jnp.take(rpb, jnp.asarray(dr.reshape(-1)), axis=1) * LOG2E
    r = r.reshape(nh, 3, NA_ROWS_PER_BLOCK, NA_WIN_ROWS, ndc)
    r = jnp.where(jnp.asarray(rv)[None, ..., None], r, NEG_BIG)
    r = jnp.concatenate([r, jnp.full(r.shape[:-1] + (1,), NEG_BIG, F32)], axis=-1)
    r = r.reshape(nh, 3, NA_ROWS_PER_BLOCK, NA_WIN_ROWS // 2, 2 * nd)
    full = jnp.einsum("htapk,kcz->htpacz", r, jnp.asarray(onehot), precision=lax.Precision.HIGHEST)
    return full.reshape(nh, 3, NA_WIN_ROWS // 2, NA_ROWS_PER_BLOCK * GRID_W, 2 * GRID_W)


def _lane_fold(x, op):
    acc = x[:, :128]
    for j in range(1, x.shape[1] // 128):
        acc = op(acc, x[:, 128 * j:128 * (j + 1)])
    return acc


def _online_update(state, s, v):
    v1 = jnp.concatenate([v, jnp.ones_like(v)], axis=1)
    mc = jnp.max(_lane_fold(s, jnp.maximum), axis=-1, keepdims=True)
    if state is None:
        return mc, _dot(jnp.exp2(s - mc).astype(BF16), v1)
    m, acc = state
    m_new = jnp.maximum(m, mc)
    p = jnp.exp2(s - m_new)
    return m_new, acc * jnp.exp2(m - m_new) + _dot(p.astype(BF16), v1)


def _softmax_finish(state):
    _, acc = state
    return acc[:, :128] / acc[:, 128:]


def _na_kernel(q_ref, k_ref, v_ref, kc_ref, vc_ref, bias_ref, o_ref, *, nblk):
    jb = pl.program_id(1)
    start = pl.multiple_of(jnp.clip(jb - 1, 0, nblk - 3) * 256, 256)
    nwin = NA_WIN_ROWS * GRID_W
    def scores(h):
        cols = slice(128 * h, 128 * (h + 1))
        q = q_ref[0, :, cols]
        bias = jnp.concatenate([bias_ref[h, 0, p] for p in range(bias_ref.shape[2])], axis=1)
        return _dot_nt(q, k_ref[0, pl.ds(start, nwin), cols]) + bias, _dot_nt(q, kc_ref[0, :, cols])

    s_next = scores(0)
    for h in range(N_HEADS):
        (s_w, s_c), s_next = s_next, (scores(h + 1) if h + 1 < N_HEADS else None)
        cols = slice(128 * h, 128 * (h + 1))
        state = _online_update(None, s_w, v_ref[0, pl.ds(start, nwin), cols])
        state = _online_update(state, s_c, vc_ref[0, :, cols])
        o_ref[0, :, cols] = _softmax_finish(state).astype(BF16)


def _na_attention(pa, pa_ctx, bias):
    b, s, _ = pa.shape
    c = pa_ctx.shape[1]
    rows = s // GRID_W
    nblk = rows // NA_ROWS_PER_BLOCK
    tq = NA_ROWS_PER_BLOCK * GRID_W
    nwin = NA_WIN_ROWS * GRID_W
    hw = N_HEADS * 128

    def bias_map(i, j):
        ty = (j > 0).astype(jnp.int32) + (j == nblk - 1).astype(jnp.int32)
        return (0, ty, 0, 0, 0)

    return pl.pallas_call(
        functools.partial(_na_kernel, nblk=nblk),
        grid=(b, nblk),
        in_specs=[
            pl.BlockSpec((1, tq, hw), lambda i, j: (i, j, 0)),
            pl.BlockSpec((1, s, hw), lambda i, j: (i, 0, 1)),
            pl.BlockSpec((1, s, hw), lambda i, j: (i, 0, 2)),
            pl.BlockSpec((1, c, hw), lambda i, j: (i, 0, 1)),
            pl.BlockSpec((1, c, hw), lambda i, j: (i, 0, 2)),
            pl.BlockSpec((N_HEADS, 1, nwin // 128, tq, 128), bias_map),
        ],
        out_specs=pl.BlockSpec((1, tq, hw), lambda i, j: (i, j, 0)),
        out_shape=jax.ShapeDtypeStruct((b, s, hw), BF16),
        compiler_params=_params("parallel", "arbitrary"),
        name="na_attention",
    )(pa, pa, pa, pa_ctx, pa_ctx, bias)


def _key_chunks(krefs, vrefs):
    chunks = []
    for k_ref, v_ref in zip(krefs, vrefs):
        n = k_ref.shape[1]
        chunks += [(k_ref, v_ref, st, min(BK, n - st)) for st in range(0, n, BK)]
    return chunks


def _head_cols(ref, h, width):
    return slice(0, width) if ref.shape[2] == width else slice(h * width, (h + 1) * width)


def _chained_softmax(chunks, scores, values):
    states = None
    for c in range(len(chunks)):
        s_cur = scores(c)
        states = states or [None] * len(s_cur)
        for i, s in enumerate(s_cur):
            states[i] = _online_update(states[i], s, values(c, i))
    return states


def _attn_kernel(q_ref, *refs, dk):
    o_ref = refs[-1]
    chunks = _key_chunks(refs[0:-1:2], refs[1:-1:2])

    def scores(c):
        k_ref, _, st, sz = chunks[c]
        return [_dot_nt(q_ref[0, :, h * dk:(h + 1) * dk], k_ref[0, st:st + sz, _head_cols(k_ref, h, dk)])
                for h in range(HEADS_PER_STEP)]

    def values(c, h):
        _, v_ref, st, sz = chunks[c]
        return v_ref[0, st:st + sz, _head_cols(v_ref, h, 128)]

    states = _chained_softmax(chunks, scores, values)
    for h in range(HEADS_PER_STEP):
        o_ref[0, :, 128 * h:128 * (h + 1)] = _softmax_finish(states[h]).astype(BF16)


def _diff_kernel(q_ref, *refs, lam_init):
    lq1_ref, lk1_ref, lq2_ref, lk2_ref, g_ref, o_ref = refs[-6:]
    chunks = _key_chunks(refs[0:-6:2], refs[1:-6:2])
    lam = (jnp.exp(jnp.sum(lq1_ref[...] * lk1_ref[...], axis=-1, keepdims=True))
           - jnp.exp(jnp.sum(lq2_ref[...] * lk2_ref[...], axis=-1, keepdims=True)) + lam_init)
    lane = lax.broadcasted_iota(jnp.int32, (1, 128), 1)
    low = (lane < DIFF_D).astype(BF16)
    qs = []
    for h in range(HEADS_PER_STEP):
        q = q_ref[0, :, 128 * h:128 * (h + 1)]
        qs += [q * low, q * (1.0 - low).astype(BF16)]

    def scores(c):
        k_ref, _, st, sz = chunks[c]
        return [_dot_nt(q, k_ref[0, st:st + sz, 128 * (i // 2):128 * (i // 2 + 1)]) for i, q in enumerate(qs)]

    def values(c, i):
        _, v_ref, st, sz = chunks[c]
        return v_ref[0, st:st + sz, 128 * (i // 2):128 * (i // 2 + 1)]

    states = _chained_softmax(chunks, scores, values)
    for h in range(HEADS_PER_STEP):
        o = _softmax_finish(states[2 * h]) - lam * _softmax_finish(states[2 * h + 1])
        o_ref[0, :, 128 * h:128 * (h + 1)] = (_rms(o, g_ref[...]) * (1.0 - lam_init)).astype(BF16)


def _attention(name, pq, p_lat, p_ctx, *, dk, k_off, v_off, shared_kv=False, diff=None):
    b, t, _ = pq.shape
    tq = min(TQ, t)
    hp = HEADS_PER_STEP
    kw, vw = (dk, 128) if shared_kv else (hp * dk, hp * 128)
    in_specs = [pl.BlockSpec((1, tq, hp * dk), lambda i, g, j: (i, j, g))]
    operands = [pq]
    for p in (p_lat, p_ctx):
        if p is None:
            continue
        n = p.shape[1]
        in_specs.append(pl.BlockSpec((1, n, kw), lambda i, g, j: (i, 0, k_off // kw + g)))
        in_specs.append(pl.BlockSpec((1, n, vw), lambda i, g, j: (i, 0, v_off // vw + g)))
        operands += [p, p]
    if diff is None:
        kernel = functools.partial(_attn_kernel, dk=dk)
    else:
        lam_vecs, g_sub, lam_init = diff
        kernel = functools.partial(_diff_kernel, lam_init=lam_init)
        in_specs += [_const_spec((1, DIFF_D))] * 4 + [_const_spec((1, 128))]
        operands += [v.reshape(1, DIFF_D) for v in lam_vecs] + [g_sub.reshape(1, 128)]
    return pl.pallas_call(
        kernel,
        grid=(b, N_HEADS // hp, t // tq),
        in_specs=in_specs,
        out_specs=pl.BlockSpec((1, tq, hp * 128), lambda i, g, j: (i, j, g)),
        out_shape=jax.ShapeDtypeStruct((b, t, N_HEADS * 128), BF16),
        compiler_params=_params("parallel", "parallel", "arbitrary"),
        name=name,
    )(*operands)


def _out_kernel(oa_ref, ob_ref, oc_ref, od_ref, w_ref, x_ref, gt_ref, g_ref, sc_ref, sh_ref,
                xo_ref, ho_ref, *, ctx_row):
    gate, sc, sh = _mod_row(gt_ref, ctx_row), _mod_row(sc_ref, ctx_row), _mod_row(sh_ref, ctx_row)
    for rows in _row_subtiles(x_ref.shape[1]):
        y = _dot(oa_ref[0, rows, :], w_ref[0, 0:512, :])
        y = y + _dot(ob_ref[0, rows, :], w_ref[0, 512:1024, :])
        y = y + _dot(oc_ref[0, rows, :], w_ref[0, 1024:1536, :])
        y = y + _dot(od_ref[0, rows, :], w_ref[0, 1536:2048, :])
        xn = x_ref[0, rows, :] + gate * y
        xo_ref[0, rows, :] = xn
        ho_ref[0, rows, :] = _modnorm(xn, g_ref[...], sc, sh).astype(BF16)


def _out_proj(outs, w_out, x, g_ffn, mod, layer, ctx_row):
    b, t, d = x.shape
    tm = min(TM_DENSE, t)
    o_spec = pl.BlockSpec((1, tm, 512), lambda i, j: (i, j, 0))
    x_spec = pl.BlockSpec((1, tm, d), lambda i, j: (i, j, 0))
    return pl.pallas_call(
        functools.partial(_out_kernel, ctx_row=ctx_row),
        grid=(b, t // tm),
        in_specs=[o_spec] * 4 + [
            pl.BlockSpec((1,) + w_out.shape[1:], lambda i, j: (layer, 0, 0)),
            x_spec,
            _mod_spec(layer, 2),
            _const_spec((1, d)),
            _mod_spec(layer, 4),
            _mod_spec(layer, 3),
        ],
        out_specs=[x_spec, x_spec],
        out_shape=[jax.ShapeDtypeStruct((b, t, d), F32), jax.ShapeDtypeStruct((b, t, d), BF16)],
        compiler_params=_params("parallel", "parallel"),
        name="out_proj",
    )(*outs, w_out, x, mod, g_ffn.reshape(1, d), mod, mod)


def _ffn_up_kernel(h_ref, wg_ref, wu_ref, *rest, cast_wd):
    if cast_wd:
        wdi_ref, o_ref, wdo_ref, wgb_ref, wub_ref = rest
        wdo_ref[...] = wdi_ref[0].astype(BF16)
    else:
        o_ref, wgb_ref, wub_ref = rest

    @pl.when((pl.program_id(1) == 0) & (pl.program_id(2) == 0))
    def _():
        wgb_ref[...] = wg_ref[0].astype(BF16)
        wub_ref[...] = wu_ref[0].astype(BF16)

    for rows in _row_subtiles(h_ref.shape[1]):
        h = h_ref[0, rows, :]
        gate = _dot(h, wgb_ref[...])
        up = _dot(h, wub_ref[...])
        o_ref[0, rows, :] = ((gate / (1.0 + jnp.exp(-gate))) * up).astype(BF16)


def _ffn_up(h, wg, wu, layer, wd=None):
    b, t, d = h.shape
    dff = wg.shape[2]
    tm = min(TM_UP, t)
    nb, nt = b, t // tm
    w_spec = pl.BlockSpec((1, d, TF), lambda f, i, j: (layer, 0, f))
    in_specs = [pl.BlockSpec((1, tm, d), lambda f, i, j: (i, j, 0)), w_spec, w_spec]
    out_specs = [pl.BlockSpec((1, tm, TF), lambda f, i, j: (i, j, f))]
    out_shape = [jax.ShapeDtypeStruct((b, t, dff), BF16)]
    operands = [h, wg, wu]
    if wd is not None:
        slab = dff // (dff // TF * nb * nt)
        in_specs.append(pl.BlockSpec((1, slab, d), lambda f, i, j: (layer, (f * nb + i) * nt + j, 0)))
        out_specs.append(pl.BlockSpec((slab, d), lambda f, i, j: ((f * nb + i) * nt + j, 0)))
        out_shape.append(jax.ShapeDtypeStruct((dff, d), BF16))
        operands.append(wd)
    outs = pl.pallas_call(
        functools.partial(_ffn_up_kernel, cast_wd=wd is not None),
        grid=(dff // TF, nb, nt),
        in_specs=in_specs,
        out_specs=out_specs,
        out_shape=out_shape,
        scratch_shapes=[pltpu.VMEM((d, TF), BF16), pltpu.VMEM((d, TF), BF16)],
        compiler_params=_params("arbitrary", "arbitrary", "arbitrary"),
        name="ffn_up",
    )(*operands)
    return outs if wd is not None else outs[0]


def _ffn_down_kernel(a_ref, wd_ref, x_ref, gt_ref, g_ref, *rest, ctx_row, final):
    xn = x_ref[0] + _mod_row(gt_ref, ctx_row) * _dot(a_ref[0], wd_ref[...])
    if final:
        (o_ref,) = rest
        o_ref[0] = _rms(xn, g_ref[...])
    else:
        sc_ref, sh_ref, xo_ref, ho_ref = rest
        xo_ref[0] = xn
        ho_ref[0] = _modnorm(xn, g_ref[...], _mod_row(sc_ref, ctx_row),
                             _mod_row(sh_ref, ctx_row)).astype(BF16)


def _ffn_down(act, wd, x, g_next, mod, layer, ctx_row, final):
    b, t, d = x.shape
    dff = wd.shape[0]
    tm = min(TM_DOWN, t)
    row_spec = pl.BlockSpec((1, tm, d), lambda i, j: (i, j, 0))
    in_specs = [
        pl.BlockSpec((1, tm, dff), lambda i, j: (i, j, 0)),
        pl.BlockSpec((dff, d), lambda i, j: (0, 0), pipeline_mode=pl.Buffered(1)),
        row_spec,
        _mod_spec(layer, 5),
        _const_spec((1, d)),
    ]
    operands = [act, wd, x, mod, g_next.reshape(1, d)]
    if final:
        out_specs = row_spec
        out_shape = jax.ShapeDtypeStruct((b, t, d), F32)
    else:
        in_specs += [_mod_spec(layer + 1, 1), _mod_spec(layer + 1, 0)]
        operands += [mod, mod]
        out_specs = [row_spec, row_spec]
        out_shape = [jax.ShapeDtypeStruct((b, t, d), F32), jax.ShapeDtypeStruct((b, t, d), BF16)]
    return pl.pallas_call(
        functools.partial(_ffn_down_kernel, ctx_row=ctx_row, final=final),
        grid=(b, t // tm),
        in_specs=in_specs,
        out_specs=out_specs,
        out_shape=out_shape,
        compiler_params=_params("parallel", "parallel"),
        name="ffn_down",
    )(*operands)


def _ffn(h, wg, wu, wd, x, g_next, mod, layer, ctx_row, final):
    if wd.ndim == 3:
        act, wd = _ffn_up(h, wg, wu, layer, wd)
    else:
        act = _ffn_up(h, wg, wu, layer)
    return _ffn_down(act, wd, x, g_next, mod, layer, ctx_row, final), wd


def _rope_tables(s, n_ctx):
    t = jnp.arange(s, dtype=jnp.int32)
    row = (t // GRID_W).astype(F32)
    col = (t % GRID_W).astype(F32)

    def cos_sin(dim):
        d_axis = dim // 2
        inv = 1.0 / (ROPE_THETA ** (jnp.arange(0, d_axis, 2, dtype=F32) / d_axis))
        ang = jnp.concatenate([row[:, None] * inv, col[:, None] * inv], axis=-1)
        return jnp.cos(ang), jnp.sin(ang)

    c128, s128 = cos_sin(HEAD_DIM)
    c64, s64 = cos_sin(MLA_ROPE)
    z = jnp.zeros_like(s64)
    lat = (
        jnp.concatenate([c128, c128], axis=-1),
        jnp.concatenate([-s128, s128], axis=-1),
        jnp.tile(c64, (1, 4)),
        jnp.tile(jnp.concatenate([-s64, z], axis=-1), (1, 2)),
        jnp.tile(jnp.concatenate([z, s64], axis=-1), (1, 2)),
    )
    one = jnp.ones((n_ctx, 128), F32)
    zero = jnp.zeros((n_ctx, 128), F32)
    return lat, (one, zero, one, zero, zero)


def kernel(x, c, ctx, c_ctx, w_ada, b_ada, g_attn, g_ffn, w_in, w_out, na_rpb, gqa_gq, gqa_gk,
           mla_gq, mla_gkv, mla_wuq, mla_wukv, diff_lq1, diff_lk1, diff_lq2, diff_lk2, diff_gsub,
           ffn_wg, ffn_wu, ffn_wd, g_final):
    depth = w_ada.shape[0]
    b, s, d = x.shape
    n_ctx = ctx.shape[1]
    rows = s // GRID_W
    ctx_row = b

    mod = _modulation(c, c_ctx, w_ada, b_ada)
    w_all = _pack_in_proj_weight(w_in)
    wo = w_out.astype(BF16)
    wg, wu = ffn_wg, ffn_wu
    tabs_lat, tabs_ctx = _rope_tables(s, b * n_ctx)

    def per_sample(a):
        return a.reshape(b, n_ctx, a.shape[-1])

    def flat(a):
        return a.reshape(1, b * n_ctx, a.shape[-1])

    xl, xc = x, flat(ctx)
    hl, hc = xl, xc

    for l in range(depth):
        ctx_out = l < depth - 1
        lam_init = 0.8 - 0.6 * math.exp(-0.3 * l)
        wuq = jnp.pad(mla_wuq[l].astype(BF16).reshape(MLA_Q_RANK, N_HEADS, MLA_NOPE + MLA_ROPE),
                      ((0, 0), (0, 0), (0, 256 - MLA_NOPE - MLA_ROPE))).reshape(MLA_Q_RANK, N_HEADS * 256)
        wukv = mla_wukv[l].astype(BF16).reshape(MLA_KV_RANK, N_HEADS, 2, 128)
        wukv = wukv.transpose(0, 2, 1, 3).reshape(MLA_KV_RANK, 2 * N_HEADS * 128)
        gq, gk = gqa_gq[l].reshape(1, 128), gqa_gk[l].reshape(1, 128)
        mgq, mgkv = mla_gq[l].reshape(1, MLA_Q_RANK), mla_gkv[l].reshape(1, MLA_KV_RANK)
        bias = _na_bias(na_rpb[l], rows)
        diff = ((diff_lq1[l], diff_lk1[l], diff_lq2[l], diff_lk2[l]), diff_gsub[l], lam_init)

        small = [gq, gk, mgq, mgkv, wuq, wukv]
        norm = (g_attn[0], mod) if l == 0 else None
        pa, pb, pc, pd = _in_proj(hl, w_all, l, small, tabs_lat, norm)
        pa_c, pb_c, pc_c, pd_c = (per_sample(p) for p in _in_proj(hc, w_all, l, small, tabs_ctx, norm, ctx_row))

        gqa_idx = dict(dk=128, k_off=512, v_off=768, shared_kv=True)
        mla_idx = dict(dk=256, k_off=1024, v_off=2048)
        std_idx = dict(dk=128, k_off=512, v_off=1024)

        outs = (
            _na_attention(pa, pa_c, bias),
            _attention("gqa_attention", pb, pb, pb_c, **gqa_idx),
            _attention("mla_attention", pc, pc, pc_c, **mla_idx),
            _attention("diff_attention", pd, pd, pd_c, diff=diff, **std_idx),
        )
        xl, hl = _out_proj(outs, wo, xl, g_ffn[l], mod, l, None)
        if ctx_out:
            outs_c = (
                _attention("na_ctx_attention", pa_c, None, pa_c, **std_idx),
                _attention("gqa_ctx_attention", pb_c, None, pb_c, **gqa_idx),
                _attention("mla_ctx_attention", pc_c, None, pc_c, **mla_idx),
                _attention("diff_ctx_attention", pd_c, None, pd_c, diff=diff, **std_idx),
            )
            xc, hc = _out_proj([flat(o) for o in outs_c], wo, xc, g_ffn[l], mod, l, ctx_row)
            (xl, hl), wd = _ffn(hl, wg, wu, ffn_wd, xl, g_attn[l + 1], mod, l, None, False)
            (xc, hc), _ = _ffn(hc, wg, wu, wd, xc, g_attn[l + 1], mod, l, ctx_row, False)
        else:
            xl, _ = _ffn(hl, wg, wu, ffn_wd, xl, g_final, mod, l, None, True)
    return xl
```

```python
import functools
import math

import numpy as np
import jax
import jax.numpy as jnp
from jax import lax
from jax.experimental import pallas as pl
from jax.experimental.pallas import tpu as pltpu

F32 = jnp.float32
BF16 = jnp.bfloat16

D_MODEL = 2048
GRID_W = 64
HEAD_DIM = 128
N_HEADS = 4
ROPE_THETA = 10000.0
EPS = 1e-6
NA_KH = 8
NA_KW = 16
NA_ROWS_PER_BLOCK = 4
NA_WIN_ROWS = 12
GQA_KV_HEADS = 2
MLA_Q_RANK = 512
MLA_KV_RANK = 256
MLA_NOPE = 128
MLA_ROPE = 64
DIFF_D = 64
D_FF = 5632

LOG2E = math.log2(math.e)
NA_SCALE = HEAD_DIM ** -0.5 * LOG2E
GQA_SCALE = HEAD_DIM ** -0.5 * LOG2E
MLA_SCALE = (MLA_NOPE + MLA_ROPE) ** -0.5 * LOG2E
DIFF_SCALE = DIFF_D ** -0.5 * LOG2E
NEG_BIG = -1e30

OFF_D = 3392

V7X_VMEM_LIMIT = 56 * 1024 * 1024

TQ = 512
HEADS_PER_STEP = 2
BK = 2048
TM_DENSE = 512
TM_UP = 2048
TM_DOWN = 256
TF = 512
ROW_SUB = 256


def _row_subtiles(tm):
    rs = min(ROW_SUB, tm)
    return [slice(r, r + rs) for r in range(0, tm, rs)]


def _params(*sem):
    return pltpu.CompilerParams(dimension_semantics=sem, vmem_limit_bytes=V7X_VMEM_LIMIT)


def _dot(a, b):
    return jnp.dot(a, b, preferred_element_type=F32)


def _dot_nt(a, b):
    return lax.dot_general(a, b, (((1,), (1,)), ((), ())), preferred_element_type=F32)


def _mod_row(ref, ctx_row):
    r = pl.program_id(0) if ctx_row is None else ctx_row
    return ref[0, 0, pl.ds(r, 1), :]


def _rms(x, g):
    return x * lax.rsqrt(jnp.mean(x * x, axis=-1, keepdims=True) + EPS) * g


def _modnorm(x, g, sc, sh):
    return _rms(x, g) * (1.0 + sc) + sh


def _rope128(x, cos, sin):
    return x * cos + pltpu.roll(x, 64, 1) * sin


def _rope64(x, c, sa, sb):
    return x * c + pltpu.roll(x, 96, 1) * sa + pltpu.roll(x, 32, 1) * sb


def _mod_kernel(cc_ref, w_ref, b_ref, o_ref):
    cc = cc_ref[...]
    s = cc / (1.0 + jnp.exp(-cc))
    o_ref[0, 0] = _dot(s.astype(BF16), w_ref[0].astype(BF16)) + b_ref[0]


def _modulation(c, c_ctx, w_ada, b_ada):
    depth, d, n = w_ada.shape
    b = c.shape[0]
    assert b < 8
    cc = jnp.zeros((8, d), F32).at[:b].set(c).at[b].set(c_ctx)
    tn = 1024
    per = d // tn
    return pl.pallas_call(
        _mod_kernel,
        grid=(depth, n // tn),
        in_specs=[
            pl.BlockSpec((8, d), lambda l, j: (0, 0)),
            pl.BlockSpec((1, d, tn), lambda l, j: (l, 0, j)),
            pl.BlockSpec((1, 1, tn), lambda l, j: (l, 0, j)),
        ],
        out_specs=pl.BlockSpec((1, 1, 8, tn), lambda l, j: (l, j // per, 0, j % per)),
        out_shape=jax.ShapeDtypeStruct((depth, n // d, 8, d), F32),
        compiler_params=_params("parallel", "parallel"),
        name="adaln_mod",
    )(cc, w_ada, b_ada.reshape(depth, 1, n))


def _mod_spec(layer, chunk):
    return pl.BlockSpec((1, 1, 8, D_MODEL), lambda *_: (layer, chunk, 0, 0))


W_COLS = (0, 1536, 2560, 3456, 4992)
PACK_ROWS = 256


def _pack_kernel(wt_ref, o_ref):
    kpe0 = OFF_D - MLA_ROPE
    rows = wt_ref.shape[2]
    o_ref[0, :, :kpe0] = wt_ref[0, :kpe0, :].T.astype(BF16)
    kpe = jnp.concatenate([wt_ref[0, kpe0:OFF_D, :], jnp.zeros((128 - MLA_ROPE, rows), F32)], axis=0)
    o_ref[0, :, kpe0:W_COLS[3]] = kpe.T.astype(BF16)
    o_ref[0, :, W_COLS[3]:] = wt_ref[0, OFF_D:, :].T.astype(BF16)


def _pack_in_proj_weight(w_in):
    depth, d, n = w_in.shape
    n_out = W_COLS[-1]
    return pl.pallas_call(
        _pack_kernel,
        grid=(depth, d // PACK_ROWS),
        in_specs=[pl.BlockSpec((1, n, PACK_ROWS), lambda l, i: (l, 0, i))],
        out_specs=pl.BlockSpec((1, PACK_ROWS, n_out), lambda l, i: (l, i, 0)),
        out_shape=jax.ShapeDtypeStruct((depth, d, n_out), BF16),
        compiler_params=_params("parallel", "parallel"),
        name="pack_in_proj_weight",
    )(jnp.swapaxes(w_in, 1, 2))
IN_PROJ_WIDTHS = (1536, 1024, 2560, 1536)


def _epi_na(y, rows, o_ref):
    o_ref[0, rows, :512] = (y[:, :512] * NA_SCALE).astype(BF16)
    o_ref[0, rows, 512:] = y[:, 512:].astype(BF16)


def _epi_gqa(y, rows, gq_ref, gk_ref, cos, sin, o_ref):
    for j in range(N_HEADS + GQA_KV_HEADS):
        g = gq_ref[...] if j < N_HEADS else gk_ref[...]
        r = _rope128(_rms(y[:, 128 * j:128 * (j + 1)], g), cos, sin)
        if j < N_HEADS:
            r = r * GQA_SCALE
        o_ref[0, rows, 128 * j:128 * (j + 1)] = r.astype(BF16)
    o_ref[0, rows, 768:] = y[:, 768:].astype(BF16)


def _epi_mla(y, rows, gq_ref, gkv_ref, wuq_ref, wukv_ref, t64, o_ref):
    q = _dot(_rms(y[:, :512], gq_ref[...]).astype(BF16), wuq_ref[...])
    kv = _dot(_rms(y[:, 512:768], gkv_ref[...]).astype(BF16), wukv_ref[...])
    kpe = _rope64(y[:, 768:896], *t64).astype(BF16)
    for h in range(N_HEADS):
        o_ref[0, rows, 256 * h:256 * h + 128] = (q[:, 256 * h:256 * h + 128] * MLA_SCALE).astype(BF16)
        qpe = _rope64(q[:, 256 * h + 128:256 * h + 256], *t64)
        o_ref[0, rows, 256 * h + 128:256 * h + 256] = (qpe * MLA_SCALE).astype(BF16)
        o_ref[0, rows, 1024 + 256 * h:1024 + 256 * h + 128] = kv[:, 128 * h:128 * (h + 1)].astype(BF16)
        o_ref[0, rows, 1024 + 256 * h + 128:1024 + 256 * h + 256] = kpe
    o_ref[0, rows, 2048:] = kv[:, 512:].astype(BF16)


def _epi_diff(y, rows, t64, o_ref):
    for j in range(2 * N_HEADS):
        r = _rope64(y[:, 128 * j:128 * (j + 1)], *t64)
        if j < N_HEADS:
            r = r * DIFF_SCALE
        o_ref[0, rows, 128 * j:128 * (j + 1)] = r.astype(BF16)
    o_ref[0, rows, 1024:] = y[:, 1024:].astype(BF16)


def _in_proj_kernel(*refs, norm, ctx_row):
    if norm:
        h_ref, g_ref, sc_ref, sh_ref, *refs = refs
        g, sc, sh = g_ref[...], _mod_row(sc_ref, ctx_row), _mod_row(sh_ref, ctx_row)
    else:
        h_ref, *refs = refs
    (w_ref, gq_ref, gk_ref, mgq_ref, mgkv_ref, wuq_ref, wukv_ref,
     cos_ref, sin_ref, c_ref, sa_ref, sb_ref, oa_ref, ob_ref, oc_ref, od_ref) = refs
    a0, b0, c0, d0, end = W_COLS
    for rows in _row_subtiles(h_ref.shape[1]):
        h = h_ref[0, rows, :]
        if norm:
            h = _modnorm(h, g, sc, sh).astype(BF16)
        t64 = (c_ref[rows, :], sa_ref[rows, :], sb_ref[rows, :])
        _epi_na(_dot(h, w_ref[0, :, a0:b0]), rows, oa_ref)
        _epi_gqa(_dot(h, w_ref[0, :, b0:c0]), rows, gq_ref, gk_ref, cos_ref[rows, :], sin_ref[rows, :], ob_ref)
        _epi_mla(_dot(h, w_ref[0, :, c0:d0]), rows, mgq_ref, mgkv_ref, wuq_ref, wukv_ref, t64, oc_ref)
        _epi_diff(_dot(h, w_ref[0, :, d0:end]), rows, t64, od_ref)


def _const_spec(shape):
    return pl.BlockSpec(shape, lambda *_: (0,) * len(shape))


def _in_proj(h, w_all, layer, small, tables, norm=None, ctx_row=None):
    b, t, d = h.shape
    tm = min(TM_DENSE, t)
    w_spec = pl.BlockSpec((1,) + w_all.shape[1:], lambda i, j: (layer, 0, 0), pipeline_mode=pl.Buffered(1))
    in_specs = [pl.BlockSpec((1, tm, d), lambda i, j: (i, j, 0))]
    operands = [h]
    if norm is not None:
        g, mod = norm
        in_specs += [_const_spec((1, d)), _mod_spec(layer, 1), _mod_spec(layer, 0)]
        operands += [g.reshape(1, d), mod, mod]
    in_specs += [w_spec] + [_const_spec(s.shape) for s in small]
    in_specs += [pl.BlockSpec((tm, 128), lambda i, j: (j, 0)) for _ in tables]
    return pl.pallas_call(
        functools.partial(_in_proj_kernel, norm=norm is not None, ctx_row=ctx_row),
        grid=(b, t // tm),
        in_specs=in_specs,
        out_specs=[pl.BlockSpec((1, tm, n), lambda i, j: (i, j, 0)) for n in IN_PROJ_WIDTHS],
        out_shape=[jax.ShapeDtypeStruct((b, t, n), BF16) for n in IN_PROJ_WIDTHS],
        compiler_params=_params("parallel", "parallel"),
        name="in_proj",
    )(*operands, w_all, *small, *tables)


def _na_index_tables(rows):
    nblk = rows // NA_ROWS_PER_BLOCK
    dr = np.zeros((3, NA_ROWS_PER_BLOCK, NA_WIN_ROWS), np.int32)
    rv = np.zeros((3, NA_ROWS_PER_BLOCK, NA_WIN_ROWS), bool)
    for ty, jb in enumerate((0, 1, nblk - 1)):
        ws = NA_ROWS_PER_BLOCK * min(max(jb - 1, 0), nblk - 3)
        for a in range(NA_ROWS_PER_BLOCK):
            r = NA_ROWS_PER_BLOCK * jb + a
            r_start = min(max(r - NA_KH // 2, 0), rows - NA_KH)
            for kap in range(NA_WIN_ROWS):
                kr = ws + kap
                ok = r_start <= kr < r_start + NA_KH
                rv[ty, a, kap] = ok
                dr[ty, a, kap] = (kr - r + NA_KH - 1) if ok else 0
    cols = np.arange(GRID_W)
    c_start = np.clip(cols - NA_KW // 2, 0, GRID_W - NA_KW)
    kc = np.arange(GRID_W)[None, :]
    cv = (kc >= c_start[:, None]) & (kc < c_start[:, None] + NA_KW)
    dc = np.where(cv, kc - cols[:, None] + NA_KW - 1, 0).astype(np.int32)
    return dr, rv, dc, cv


def _na_bias(rpb, rows):
    dr, rv, dc, cv = _na_index_tables(rows)
    nh, _, ndc = rpb.shape
    nd = ndc + 1
    onehot = np.zeros((2, nd, GRID_W, 2, GRID_W), np.float32)
    for e in range(2):
        onehot[e, np.where(cv, dc, ndc), np.arange(GRID_W)[:, None], e, np.arange(GRID_W)[None, :]] = 1.0
    onehot = onehot.reshape(2 * nd, GRID_W, 2 * GRID_W)
    r = jnp.take(rpb, jnp.asarray(dr.reshape(-1)), axis=1) * LOG2E
    r = r.reshape(nh, 3, NA_ROWS_PER_BLOCK, NA_WIN_ROWS, ndc)
    r = jnp.where(jnp.asarray(rv)[None, ..., None], r, NEG_BIG)
    r = jnp.concatenate([r, jnp.full(r.shape[:-1] + (1,), NEG_BIG, F32)], axis=-1)
    r = r.reshape(nh, 3, NA_ROWS_PER_BLOCK, NA_WIN_ROWS // 2, 2 * nd)
    full = jnp.einsum("htapk,kcz->htpacz", r, jnp.asarray(onehot), precision=lax.Precision.HIGHEST)
    return full.reshape(nh, 3, NA_WIN_ROWS // 2, NA_ROWS_PER_BLOCK * GRID_W, 2 * GRID_W)


def _lane_fold(x, op):
    acc = x[:, :128]
    for j in range(1, x.shape[1] // 128):
        acc = op(acc, x[:, 128 * j:128 * (j + 1)])
    return acc


def _online_update(state, s, v):
    v1 = jnp.concatenate([v, jnp.ones_like(v)], axis=1)
    mc = jnp.max(_lane_fold(s, jnp.maximum), axis=-1, keepdims=True)
    if state is None:
        return mc, _dot(jnp.exp2(s - mc).astype(BF16), v1)
    m, acc = state
    m_new = jnp.maximum(m, mc)
    p = jnp.exp2(s - m_new)
    return m_new, acc * jnp.exp2(m - m_new) + _dot(p.astype(BF16), v1)


def _softmax_finish(state):
    _, acc = state
    return acc[:, :128] / acc[:, 128:]


def _na_kernel(q_ref, k_ref, v_ref, kc_ref, vc_ref, bias_ref, o_ref, *, nblk):
    jb = pl.program_id(1)
    start = pl.multiple_of(jnp.clip(jb - 1, 0, nblk - 3) * 256, 256)
    nwin = NA_WIN_ROWS * GRID_W
    def scores(h):
        cols = slice(128 * h, 128 * (h + 1))
        q = q_ref[0, :, cols]
        bias = jnp.concatenate([bias_ref[h, 0, p] for p in range(bias_ref.shape[2])], axis=1)
        return _dot_nt(q, k_ref[0, pl.ds(start, nwin), cols]) + bias, _dot_nt(q, kc_ref[0, :, cols])

    s_next = scores(0)
    for h in range(N_HEADS):
        (s_w, s_c), s_next = s_next, (scores(h + 1) if h + 1 < N_HEADS else None)
        cols = slice(128 * h, 128 * (h + 1))
        state = _online_update(None, s_w, v_ref[0, pl.ds(start, nwin), cols])
        state = _online_update(state, s_c, vc_ref[0, :, cols])
        o_ref[0, :, cols] = _softmax_finish(state).astype(BF16)


def _na_attention(pa, pa_ctx, bias):
    b, s, _ = pa.shape
    c = pa_ctx.shape[1]
    rows = s // GRID_W
    nblk = rows // NA_ROWS_PER_BLOCK
    tq = NA_ROWS_PER_BLOCK * GRID_W
    nwin = NA_WIN_ROWS * GRID_W
    hw = N_HEADS * 128

    def bias_map(i, j):
        ty = (j > 0).astype(jnp.int32) + (j == nblk - 1).astype(jnp.int32)
        return (0, ty, 0, 0, 0)

    return pl.pallas_call(
        functools.partial(_na_kernel, nblk=nblk),
        grid=(b, nblk),
        in_specs=[
            pl.BlockSpec((1, tq, hw), lambda i, j: (i, j, 0)),
            pl.BlockSpec((1, s, hw), lambda i, j: (i, 0, 1)),
            pl.BlockSpec((1, s, hw), lambda i, j: (i, 0, 2)),
            pl.BlockSpec((1, c, hw), lambda i, j: (i, 0, 1)),
            pl.BlockSpec((1, c, hw), lambda i, j: (i, 0, 2)),
            pl.BlockSpec((N_HEADS, 1, nwin // 128, tq, 128), bias_map),
        ],
        out_specs=pl.BlockSpec((1, tq, hw), lambda i, j: (i, j, 0)),
        out_shape=jax.ShapeDtypeStruct((b, s, hw), BF16),
        compiler_params=_params("parallel", "arbitrary"),
        name="na_attention",
    )(pa, pa, pa, pa_ctx, pa_ctx, bias)


def _key_chunks(krefs, vrefs):
    chunks = []
    for k_ref, v_ref in zip(krefs, vrefs):
        n = k_ref.shape[1]
        chunks += [(k_ref, v_ref, st, min(BK, n - st)) for st in range(0, n, BK)]
    return chunks


def _head_cols(ref, h, width):
    return slice(0, width) if ref.shape[2] == width else slice(h * width, (h + 1) * width)


def _chained_softmax(chunks, scores, values):
    states = None
    for c in range(len(chunks)):
        s_cur = scores(c)
        states = states or [None] * len(s_cur)
        for i, s in enumerate(s_cur):
            states[i] = _online_update(states[i], s, values(c, i))
    return states


def _attn_kernel(q_ref, *refs, dk):
    o_ref = refs[-1]
    chunks = _key_chunks(refs[0:-1:2], refs[1:-1:2])

    def scores(c):
        k_ref, _, st, sz = chunks[c]
        return [_dot_nt(q_ref[0, :, h * dk:(h + 1) * dk], k_ref[0, st:st + sz, _head_cols(k_ref, h, dk)])
                for h in range(HEADS_PER_STEP)]

    def values(c, h):
        _, v_ref, st, sz = chunks[c]
        return v_ref[0, st:st + sz, _head_cols(v_ref, h, 128)]

    states = _chained_softmax(chunks, scores, values)
    for h in range(HEADS_PER_STEP):
        o_ref[0, :, 128 * h:128 * (h + 1)] = _softmax_finish(states[h]).astype(BF16)


def _diff_kernel(q_ref, *refs, lam_init):
    lq1_ref, lk1_ref, lq2_ref, lk2_ref, g_ref, o_ref = refs[-6:]
    chunks = _key_chunks(refs[0:-6:2], refs[1:-6:2])
    lam = (jnp.exp(jnp.sum(lq1_ref[...] * lk1_ref[...], axis=-1, keepdims=True))
           - jnp.exp(jnp.sum(lq2_ref[...] * lk2_ref[...], axis=-1, keepdims=True)) + lam_init)
    lane = lax.broadcasted_iota(jnp.int32, (1, 128), 1)
    low = (lane < DIFF_D).astype(BF16)
    qs = []
    for h in range(HEADS_PER_STEP):
        q = q_ref[0, :, 128 * h:128 * (h + 1)]
        qs += [q * low, q * (1.0 - low).astype(BF16)]

    def scores(c):
        k_ref, _, st, sz = chunks[c]
        return [_dot_nt(q, k_ref[0, st:st + sz, 128 * (i // 2):128 * (i // 2 + 1)]) for i, q in enumerate(qs)]

    def values(c, i):
        _, v_ref, st, sz = chunks[c]
        return v_ref[0, st:st + sz, 128 * (i // 2):128 * (i // 2 + 1)]

    states = _chained_softmax(chunks, scores, values)
    for h in range(HEADS_PER_STEP):
        o = _softmax_finish(states[2 * h]) - lam * _softmax_finish(states[2 * h + 1])
        o_ref[0, :, 128 * h:128 * (h + 1)] = (_rms(o, g_ref[...]) * (1.0 - lam_init)).astype(BF16)


def _attention(name, pq, p_lat, p_ctx, *, dk, k_off, v_off, shared_kv=False, diff=None):
    b, t, _ = pq.shape
    tq = min(TQ, t)
    hp = HEADS_PER_STEP
    kw, vw = (dk, 128) if shared_kv else (hp * dk, hp * 128)
    in_specs = [pl.BlockSpec((1, tq, hp * dk), lambda i, g, j: (i, j, g))]
    operands = [pq]
    for p in (p_lat, p_ctx):
        if p is None:
            continue
        n = p.shape[1]
        in_specs.append(pl.BlockSpec((1, n, kw), lambda i, g, j: (i, 0, k_off // kw + g)))
        in_specs.append(pl.BlockSpec((1, n, vw), lambda i, g, j: (i, 0, v_off // vw + g)))
        operands += [p, p]
    if diff is None:
        kernel = functools.partial(_attn_kernel, dk=dk)
    else:
        lam_vecs, g_sub, lam_init = diff
        kernel = functools.partial(_diff_kernel, lam_init=lam_init)
        in_specs += [_const_spec((1, DIFF_D))] * 4 + [_const_spec((1, 128))]
        operands += [v.reshape(1, DIFF_D) for v in lam_vecs] + [g_sub.reshape(1, 128)]
    return pl.pallas_call(
        kernel,
        grid=(b, N_HEADS // hp, t // tq),
        in_specs=in_specs,
        out_specs=pl.BlockSpec((1, tq, hp * 128), lambda i, g, j: (i, j, g)),
        out_shape=jax.ShapeDtypeStruct((b, t, N_HEADS * 128), BF16),
        compiler_params=_params("parallel", "parallel", "arbitrary"),
        name=name,
    )(*operands)


def _out_kernel(oa_ref, ob_ref, oc_ref, od_ref, w_ref, x_ref, gt_ref, g_ref, sc_ref, sh_ref,
                xo_ref, ho_ref, *, ctx_row):
    gate, sc, sh = _mod_row(gt_ref, ctx_row), _mod_row(sc_ref, ctx_row), _mod_row(sh_ref, ctx_row)
    for rows in _row_subtiles(x_ref.shape[1]):
        y = _dot(oa_ref[0, rows, :], w_ref[0, 0:512, :])
        y = y + _dot(ob_ref[0, rows, :], w_ref[0, 512:1024, :])
        y = y + _dot(oc_ref[0, rows, :], w_ref[0, 1024:1536, :])
        y = y + _dot(od_ref[0, rows, :], w_ref[0, 1536:2048, :])
        xn = x_ref[0, rows, :] + gate * y
        xo_ref[0, rows, :] = xn
        ho_ref[0, rows, :] = _modnorm(xn, g_ref[...], sc, sh).astype(BF16)


def _out_proj(outs, w_out, x, g_ffn, mod, layer, ctx_row):
    b, t, d = x.shape
    tm = min(TM_DENSE, t)
    o_spec = pl.BlockSpec((1, tm, 512), lambda i, j: (i, j, 0))
    x_spec = pl.BlockSpec((1, tm, d), lambda i, j: (i, j, 0))
    return pl.pallas_call(
        functools.partial(_out_kernel, ctx_row=ctx_row),
        grid=(b, t // tm),
        in_specs=[o_spec] * 4 + [
            pl.BlockSpec((1,) + w_out.shape[1:], lambda i, j: (layer, 0, 0)),
            x_spec,
            _mod_spec(layer, 2),
            _const_spec((1, d)),
            _mod_spec(layer, 4),
            _mod_spec(layer, 3),
        ],
        out_specs=[x_spec, x_spec],
        out_shape=[jax.ShapeDtypeStruct((b, t, d), F32), jax.ShapeDtypeStruct((b, t, d), BF16)],
        compiler_params=_params("parallel", "parallel"),
        name="out_proj",
    )(*outs, w_out, x, mod, g_ffn.reshape(1, d), mod, mod)


def _ffn_up_kernel(h_ref, wg_ref, wu_ref, *rest, cast_wd):
    if cast_wd:
        wdi_ref, o_ref, wdo_ref, wgb_ref, wub_ref = rest
        wdo_ref[...] = wdi_ref[0].astype(BF16)
    else:
        o_ref, wgb_ref, wub_ref = rest

    @pl.when((pl.program_id(1) == 0) & (pl.program_id(2) == 0))
    def _():
        wgb_ref[...] = wg_ref[0].astype(BF16)
        wub_ref[...] = wu_ref[0].astype(BF16)

    for rows in _row_subtiles(h_ref.shape[1]):
        h = h_ref[0, rows, :]
        gate = _dot(h, wgb_ref[...])
        up = _dot(h, wub_ref[...])
        o_ref[0, rows, :] = ((gate / (1.0 + jnp.exp(-gate))) * up).astype(BF16)


def _ffn_up(h, wg, wu, layer, wd=None):
    b, t, d = h.shape
    dff = wg.shape[2]
    tm = min(TM_UP, t)
    nb, nt = b, t // tm
    w_spec = pl.BlockSpec((1, d, TF), lambda f, i, j: (layer, 0, f))
    in_specs = [pl.BlockSpec((1, tm, d), lambda f, i, j: (i, j, 0)), w_spec, w_spec]
    out_specs = [pl.BlockSpec((1, tm, TF), lambda f, i, j: (i, j, f))]
    out_shape = [jax.ShapeDtypeStruct((b, t, dff), BF16)]
    operands = [h, wg, wu]
    if wd is not None:
        slab = dff // (dff // TF * nb * nt)
        in_specs.append(pl.BlockSpec((1, slab, d), lambda f, i, j: (layer, (f * nb + i) * nt + j, 0)))
        out_specs.append(pl.BlockSpec((slab, d), lambda f, i, j: ((f * nb + i) * nt + j, 0)))
        out_shape.append(jax.ShapeDtypeStruct((dff, d), BF16))
        operands.append(wd)
    outs = pl.pallas_call(
        functools.partial(_ffn_up_kernel, cast_wd=wd is not None),
        grid=(dff // TF, nb, nt),
        in_specs=in_specs,
        out_specs=out_specs,
        out_shape=out_shape,
        scratch_shapes=[pltpu.VMEM((d, TF), BF16), pltpu.VMEM((d, TF), BF16)],
        compiler_params=_params("arbitrary", "arbitrary", "arbitrary"),
        name="ffn_up",
    )(*operands)
    return outs if wd is not None else outs[0]


def _ffn_down_kernel(a_ref, wd_ref, x_ref, gt_ref, g_ref, *rest, ctx_row, final):
    xn = x_ref[0] + _mod_row(gt_ref, ctx_row) * _dot(a_ref[0], wd_ref[...])
    if final:
        (o_ref,) = rest
        o_ref[0] = _rms(xn, g_ref[...])
    else:
        sc_ref, sh_ref, xo_ref, ho_ref = rest
        xo_ref[0] = xn
        ho_ref[0] = _modnorm(xn, g_ref[...], _mod_row(sc_ref, ctx_row),
                             _mod_row(sh_ref, ctx_row)).astype(BF16)


def _ffn_down(act, wd, x, g_next, mod, layer, ctx_row, final):
    b, t, d = x.shape
    dff = wd.shape[0]
    tm = min(TM_DOWN, t)
    row_spec = pl.BlockSpec((1, tm, d), lambda i, j: (i, j, 0))
    in_specs = [
        pl.BlockSpec((1, tm, dff), lambda i, j: (i, j, 0)),
        pl.BlockSpec((dff, d), lambda i, j: (0, 0), pipeline_mode=pl.Buffered(1)),
        row_spec,
        _mod_spec(layer, 5),
        _const_spec((1, d)),
    ]
    operands = [act, wd, x, mod, g_next.reshape(1, d)]
    if final:
        out_specs = row_spec
        out_shape = jax.ShapeDtypeStruct((b, t, d), F32)
    else:
        in_specs += [_mod_spec(layer + 1, 1), _mod_spec(layer + 1, 0)]
        operands += [mod, mod]
        out_specs = [row_spec, row_spec]
        out_shape = [jax.ShapeDtypeStruct((b, t, d), F32), jax.ShapeDtypeStruct((b, t, d), BF16)]
    return pl.pallas_call(
        functools.partial(_ffn_down_kernel, ctx_row=ctx_row, final=final),
        grid=(b, t // tm),
        in_specs=in_specs,
        out_specs=out_specs,
        out_shape=out_shape,
        compiler_params=_params("parallel", "parallel"),
        name="ffn_down",
    )(*operands)


def _ffn(h, wg, wu, wd, x, g_next, mod, layer, ctx_row, final):
    if wd.ndim == 3:
        act, wd = _ffn_up(h, wg, wu, layer, wd)
    else:
        act = _ffn_up(h, wg, wu, layer)
    return _ffn_down(act, wd, x, g_next, mod, layer, ctx_row, final), wd


def _rope_tables(s, n_ctx):
    t = jnp.arange(s, dtype=jnp.int32)
    row = (t // GRID_W).astype(F32)
    col = (t % GRID_W).astype(F32)

    def cos_sin(dim):
        d_axis = dim // 2
        inv = 1.0 / (ROPE_THETA ** (jnp.arange(0, d_axis, 2, dtype=F32) / d_axis))
        ang = jnp.concatenate([row[:, None] * inv, col[:, None] * inv], axis=-1)
        return jnp.cos(ang), jnp.sin(ang)

    c128, s128 = cos_sin(HEAD_DIM)
    c64, s64 = cos_sin(MLA_ROPE)
    z = jnp.zeros_like(s64)
    lat = (
        jnp.concatenate([c128, c128], axis=-1),
        jnp.concatenate([-s128, s128], axis=-1),
        jnp.tile(c64, (1, 4)),
        jnp.tile(jnp.concatenate([-s64, z], axis=-1), (1, 2)),
        jnp.tile(jnp.concatenate([z, s64], axis=-1), (1, 2)),
    )
    one = jnp.ones((n_ctx, 128), F32)
    zero = jnp.zeros((n_ctx, 128), F32)
    return lat, (one, zero, one, zero, zero)


def kernel(x, c, ctx, c_ctx, w_ada, b_ada, g_attn, g_ffn, w_in, w_out, na_rpb, gqa_gq, gqa_gk,
           mla_gq, mla_gkv, mla_wuq, mla_wukv, diff_lq1, diff_lk1, diff_lq2, diff_lk2, diff_gsub,
           ffn_wg, ffn_wu, ffn_wd, g_final):
    depth = w_ada.shape[0]
    b, s, d = x.shape
    n_ctx = ctx.shape[1]
    rows = s // GRID_W
    ctx_row = b

    mod = _modulation(c, c_ctx, w_ada, b_ada)
    w_all = _pack_in_proj_weight(w_in)
    wo = w_out.astype(BF16)
    wg, wu = ffn_wg, ffn_wu
    tabs_lat, tabs_ctx = _rope_tables(s, b * n_ctx)

    def per_sample(a):
        return a.reshape(b, n_ctx, a.shape[-1])

    def flat(a):
        return a.reshape(1, b * n_ctx, a.shape[-1])

    xl, xc = x, flat(ctx)
    hl, hc = xl, xc

    for l in range(depth):
        ctx_out = l < depth - 1
        lam_init = 0.8 - 0.6 * math.exp(-0.3 * l)
        wuq = jnp.pad(mla_wuq[l].astype(BF16).reshape(MLA_Q_RANK, N_HEADS, MLA_NOPE + MLA_ROPE),
                      ((0, 0), (0, 0), (0, 256 - MLA_NOPE - MLA_ROPE))).reshape(MLA_Q_RANK, N_HEADS * 256)
        wukv = mla_wukv[l].astype(BF16).reshape(MLA_KV_RANK, N_HEADS, 2, 128)
        wukv = wukv.transpose(0, 2, 1, 3).reshape(MLA_KV_RANK, 2 * N_HEADS * 128)
        gq, gk = gqa_gq[l].reshape(1, 128), gqa_gk[l].reshape(1, 128)
        mgq, mgkv = mla_gq[l].reshape(1, MLA_Q_RANK), mla_gkv[l].reshape(1, MLA_KV_RANK)
        bias = _na_bias(na_rpb[l], rows)
        diff = ((diff_lq1[l], diff_lk1[l], diff_lq2[l], diff_lk2[l]), diff_gsub[l], lam_init)

        small = [gq, gk, mgq, mgkv, wuq, wukv]
        norm = (g_attn[0], mod) if l == 0 else None
        pa, pb, pc, pd = _in_proj(hl, w_all, l, small, tabs_lat, norm)
        pa_c, pb_c, pc_c, pd_c = (per_sample(p) for p in _in_proj(hc, w_all, l, small, tabs_ctx, norm, ctx_row))

        gqa_idx = dict(dk=128, k_off=512, v_off=768, shared_kv=True)
        mla_idx = dict(dk=256, k_off=1024, v_off=2048)
        std_idx = dict(dk=128, k_off=512, v_off=1024)

        outs = (
            _na_attention(pa, pa_c, bias),
            _attention("gqa_attention", pb, pb, pb_c, **gqa_idx),
            _attention("mla_attention", pc, pc, pc_c, **mla_idx),
            _attention("diff_attention", pd, pd, pd_c, diff=diff, **std_idx),
        )
        xl, hl = _out_proj(outs, wo, xl, g_ffn[l], mod, l, None)
        if ctx_out:
            outs_c = (
                _attention("na_ctx_attention", pa_c, None, pa_c, **std_idx),
                _attention("gqa_ctx_attention", pb_c, None, pb_c, **gqa_idx),
                _attention("mla_ctx_attention", pc_c, None, pc_c, **mla_idx),
                _attention("diff_ctx_attention", pd_c, None, pd_c, diff=diff, **std_idx),
            )
            xc, hc = _out_proj([flat(o) for o in outs_c], wo, xc, g_ffn[l], mod, l, ctx_row)
            (xl, hl), wd = _ffn(hl, wg, wu, ffn_wd, xl, g_attn[l + 1], mod, l, None, False)
            (xc, hc), _ = _ffn(hc, wg, wu, wd, xc, g_attn[l + 1], mod, l, ctx_row, False)
        else:
            xl, _ = _ffn(hl, wg, wu, ffn_wd, xl, g_final, mod, l, None, True)
    return xl
```

```python
import functools
import math

import numpy as np
import jax
import jax.numpy as jnp
from jax import lax
from jax.experimental import pallas as pl
from jax.experimental.pallas import tpu as pltpu

F32 = jnp.float32
BF16 = jnp.bfloat16

D_MODEL = 2048
GRID_W = 64
HEAD_DIM = 128
N_HEADS = 4
ROPE_THETA = 10000.0
EPS = 1e-6
NA_KH = 8
NA_KW = 16
NA_ROWS_PER_BLOCK = 4
NA_WIN_ROWS = 12
GQA_KV_HEADS = 2
MLA_Q_RANK = 512
MLA_KV_RANK = 256
MLA_NOPE = 128
MLA_ROPE = 64
DIFF_D = 64
D_FF = 5632

LOG2E = math.log2(math.e)
NA_SCALE = HEAD_DIM ** -0.5 * LOG2E
GQA_SCALE = HEAD_DIM ** -0.5 * LOG2E
MLA_SCALE = (MLA_NOPE + MLA_ROPE) ** -0.5 * LOG2E
DIFF_SCALE = DIFF_D ** -0.5 * LOG2E
NEG_BIG = -1e30

OFF_D = 3392

V7X_VMEM_LIMIT = 56 * 1024 * 1024

TQ = 512
HEADS_PER_STEP = 2
BK = 2048
TM_DENSE = 512
TM_UP = 2048
TM_DOWN = 256
TF = 512
ROW_SUB = 256


def _row_subtiles(tm):
    rs = min(ROW_SUB, tm)
    return [slice(r, r + rs) for r in range(0, tm, rs)]


def _params(*sem):
    return pltpu.CompilerParams(dimension_semantics=sem, vmem_limit_bytes=V7X_VMEM_LIMIT)


def _dot(a, b):
    return jnp.dot(a, b, preferred_element_type=F32)


def _dot_nt(a, b):
    return lax.dot_general(a, b, (((1,), (1,)), ((), ())), preferred_element_type=F32)


def _mod_row(ref, ctx_row):
    r = pl.program_id(0) if ctx_row is None else ctx_row
    return ref[0, 0, pl.ds(r, 1), :]


def _rms(x, g):
    return x * lax.rsqrt(jnp.mean(x * x, axis=-1, keepdims=True) + EPS) * g


def _modnorm(x, g, sc, sh):
    return _rms(x, g) * (1.0 + sc) + sh


def _rope128(x, cos, sin):
    return x * cos + pltpu.roll(x, 64, 1) * sin


def _rope64(x, c, sa, sb):
    return x * c + pltpu.roll(x, 96, 1) * sa + pltpu.roll(x, 32, 1) * sb


def _mod_kernel(cc_ref, w_ref, b_ref, o_ref):
    cc = cc_ref[...]
    s = cc / (1.0 + jnp.exp(-cc))
    o_ref[0, 0] = _dot(s.astype(BF16), w_ref[0].astype(BF16)) + b_ref[0]


def _modulation(c, c_ctx, w_ada, b_ada):
    depth, d, n = w_ada.shape
    b = c.shape[0]
    assert b < 8
    cc = jnp.zeros((8, d), F32).at[:b].set(c).at[b].set(c_ctx)
    tn = 1024
    per = d // tn
    return pl.pallas_call(
        _mod_kernel,
        grid=(depth, n // tn),
        in_specs=[
            pl.BlockSpec((8, d), lambda l, j: (0, 0)),
            pl.BlockSpec((1, d, tn), lambda l, j: (l, 0, j)),
            pl.BlockSpec((1, 1, tn), lambda l, j: (l, 0, j)),
        ],
        out_specs=pl.BlockSpec((1, 1, 8, tn), lambda l, j: (l, j // per, 0, j % per)),
        out_shape=jax.ShapeDtypeStruct((depth, n // d, 8, d), F32),
        compiler_params=_params("parallel", "parallel"),
        name="adaln_mod",
    )(cc, w_ada, b_ada.reshape(depth, 1, n))


def _mod_spec(layer, chunk):
    return pl.BlockSpec((1, 1, 8, D_MODEL), lambda *_: (layer, chunk, 0, 0))


W_COLS = (0, 1536, 2560, 3456, 4992)
PACK_ROWS = 256


def _pack_kernel(wt_ref, wo_ref, o_ref, oo_ref):
    oo_ref[...] = wo_ref[...].astype(BF16)
    kpe0 = OFF_D - MLA_ROPE
    rows = wt_ref.shape[2]
    o_ref[0, :, :kpe0] = wt_ref[0, :kpe0, :].T.astype(BF16)
    kpe = jnp.concatenate([wt_ref[0, kpe0:OFF_D, :], jnp.zeros((128 - MLA_ROPE, rows), F32)], axis=0)
    o_ref[0, :, kpe0:W_COLS[3]] = kpe.T.astype(BF16)
    o_ref[0, :, W_COLS[3]:] = wt_ref[0, OFF_D:, :].T.astype(BF16)


def _pack_weights(w_in, w_out):
    depth, d, n = w_in.shape
    n_out = W_COLS[-1]
    slab = w_out.shape[1] * PACK_ROWS // d
    wo_spec = pl.BlockSpec((1, slab, w_out.shape[2]), lambda l, i: (l, i, 0))
    return pl.pallas_call(
        _pack_kernel,
        grid=(depth, d // PACK_ROWS),
        in_specs=[pl.BlockSpec((1, n, PACK_ROWS), lambda l, i: (l, 0, i)), wo_spec],
        out_specs=[pl.BlockSpec((1, PACK_ROWS, n_out), lambda l, i: (l, i, 0)), wo_spec],
        out_shape=[jax.ShapeDtypeStruct((depth, d, n_out), BF16), jax.ShapeDtypeStruct(w_out.shape, BF16)],
        compiler_params=_params("parallel", "parallel"),
        name="pack_weights",
    )(jnp.swapaxes(w_in, 1, 2), w_out)
IN_PROJ_WIDTHS = (1536, 1024, 2560, 1536)


def _epi_na(y, rows, o_ref):
    o_ref[0, rows, :512] = (y[:, :512] * NA_SCALE).astype(BF16)
    o_ref[0, rows, 512:] = y[:, 512:].astype(BF16)


def _epi_gqa(y, rows, gq_ref, gk_ref, cos, sin, o_ref):
    for j in range(N_HEADS + GQA_KV_HEADS):
        g = gq_ref[...] if j < N_HEADS else gk_ref[...]
        r = _rope128(_rms(y[:, 128 * j:128 * (j + 1)], g), cos, sin)
        if j < N_HEADS:
            r = r * GQA_SCALE
        o_ref[0, rows, 128 * j:128 * (j + 1)] = r.astype(BF16)
    o_ref[0, rows, 768:] = y[:, 768:].astype(BF16)


def _epi_mla(y, rows, gq_ref, gkv_ref, wuq_ref, wukv_ref, t64, o_ref):
    q = _dot(_rms(y[:, :512], gq_ref[...]).astype(BF16), wuq_ref[...])
    kv = _dot(_rms(y[:, 512:768], gkv_ref[...]).astype(BF16), wukv_ref[...])
    kpe = _rope64(y[:, 768:896], *t64).astype(BF16)
    for h in range(N_HEADS):
        o_ref[0, rows, 256 * h:256 * h + 128] = (q[:, 256 * h:256 * h + 128] * MLA_SCALE).astype(BF16)
        qpe = _rope64(q[:, 256 * h + 128:256 * h + 256], *t64)
        o_ref[0, rows, 256 * h + 128:256 * h + 256] = (qpe * MLA_SCALE).astype(BF16)
        o_ref[0, rows, 1024 + 256 * h:1024 + 256 * h + 128] = kv[:, 128 * h:128 * (h + 1)].astype(BF16)
        o_ref[0, rows, 1024 + 256 * h + 128:1024 + 256 * h + 256] = kpe
    o_ref[0, rows, 2048:] = kv[:, 512:].astype(BF16)


def _epi_diff(y, rows, t64, o_ref):
    for j in range(2 * N_HEADS):
        r = _rope64(y[:, 128 * j:128 * (j + 1)], *t64)
        if j < N_HEADS:
            r = r * DIFF_SCALE
        o_ref[0, rows, 128 * j:128 * (j + 1)] = r.astype(BF16)
    o_ref[0, rows, 1024:] = y[:, 1024:].astype(BF16)


def _in_proj_kernel(*refs, norm, ctx_row):
    if norm:
        h_ref, g_ref, sc_ref, sh_ref, *refs = refs
        g, sc, sh = g_ref[...], _mod_row(sc_ref, ctx_row), _mod_row(sh_ref, ctx_row)
    else:
        h_ref, *refs = refs
    (w_ref, gq_ref, gk_ref, mgq_ref, mgkv_ref, wuq_ref, wukv_ref,
     cos_ref, sin_ref, c_ref, sa_ref, sb_ref, oa_ref, ob_ref, oc_ref, od_ref) = refs
    a0, b0, c0, d0, end = W_COLS
    for rows in _row_subtiles(h_ref.shape[1]):
        h = h_ref[0, rows, :]
        if norm:
            h = _modnorm(h, g, sc, sh).astype(BF16)
        t64 = (c_ref[rows, :], sa_ref[rows, :], sb_ref[rows, :])
        _epi_na(_dot(h, w_ref[0, :, a0:b0]), rows, oa_ref)
        _epi_gqa(_dot(h, w_ref[0, :, b0:c0]), rows, gq_ref, gk_ref, cos_ref[rows, :], sin_ref[rows, :], ob_ref)
        _epi_mla(_dot(h, w_ref[0, :, c0:d0]), rows, mgq_ref, mgkv_ref, wuq_ref, wukv_ref, t64, oc_ref)
        _epi_diff(_dot(h, w_ref[0, :, d0:end]), rows, t64, od_ref)


def _const_spec(shape):
    return pl.BlockSpec(shape, lambda *_: (0,) * len(shape))


def _in_proj(h, w_all, layer, small, tables, norm=None, ctx_row=None):
    b, t, d = h.shape
    tm = min(TM_DENSE, t)
    w_spec = pl.BlockSpec((1,) + w_all.shape[1:], lambda i, j: (layer, 0, 0), pipeline_mode=pl.Buffered(1))
    in_specs = [pl.BlockSpec((1, tm, d), lambda i, j: (i, j, 0))]
    operands = [h]
    if norm is not None:
        g, mod = norm
        in_specs += [_const_spec((1, d)), _mod_spec(layer, 1), _mod_spec(layer, 0)]
        operands += [g.reshape(1, d), mod, mod]
    in_specs += [w_spec] + [_const_spec(s.shape) for s in small]
    in_specs += [pl.BlockSpec((tm, 128), lambda i, j: (j, 0)) for _ in tables]
    return pl.pallas_call(
        functools.partial(_in_proj_kernel, norm=norm is not None, ctx_row=ctx_row),
        grid=(b, t // tm),
        in_specs=in_specs,
        out_specs=[pl.BlockSpec((1, tm, n), lambda i, j: (i, j, 0)) for n in IN_PROJ_WIDTHS],
        out_shape=[jax.ShapeDtypeStruct((b, t, n), BF16) for n in IN_PROJ_WIDTHS],
        compiler_params=_params("parallel", "parallel"),
        name="in_proj",
    )(*operands, w_all, *small, *tables)


def _na_index_tables(rows):
    nblk = rows // NA_ROWS_PER_BLOCK
    dr = np.zeros((3, NA_ROWS_PER_BLOCK, NA_WIN_ROWS), np.int32)
    rv = np.zeros((3, NA_ROWS_PER_BLOCK, NA_WIN_ROWS), bool)
    for ty, jb in enumerate((0, 1, nblk - 1)):
        ws = NA_ROWS_PER_BLOCK * min(max(jb - 1, 0), nblk - 3)
        for a in range(NA_ROWS_PER_BLOCK):
            r = NA_ROWS_PER_BLOCK * jb + a
            r_start = min(max(r - NA_KH // 2, 0), rows - NA_KH)
            for kap in range(NA_WIN_ROWS):
                kr = ws + kap
                ok = r_start <= kr < r_start + NA_KH
                rv[ty, a, kap] = ok
                dr[ty, a, kap] = (kr - r + NA_KH - 1) if ok else 0
    cols = np.arange(GRID_W)
    c_start = np.clip(cols - NA_KW // 2, 0, GRID_W - NA_KW)
    kc = np.arange(GRID_W)[None, :]
    cv = (kc >= c_start[:, None]) & (kc < c_start[:, None] + NA_KW)
    dc = np.where(cv, kc - cols[:, None] + NA_KW - 1, 0).astype(np.int32)
    return dr, rv, dc, cv


def _na_bias(rpb, rows):
    dr, rv, dc, cv = _na_index_tables(rows)
    nh, _, ndc = rpb.shape
    nd = ndc + 1
    onehot = np.zeros((2, nd, GRID_W, 2, GRID_W), np.float32)
    for e in range(2):
        onehot[e, np.where(cv, dc, ndc), np.arange(GRID_W)[:, None], e, np.arange(GRID_W)[None, :]] = 1.0
    onehot = onehot.reshape(2 * nd, GRID_W, 2 * GRID_W)
    r = jnp.take(rpb, jnp.asarray(dr.reshape(-1)), axis=1) * LOG2E
    r = r.reshape(nh, 3, NA_ROWS_PER_BLOCK, NA_WIN_ROWS, ndc)
    r = jnp.where(jnp.asarray(rv)[None, ..., None], r, NEG_BIG)
    r = jnp.concatenate([r, jnp.full(r.shape[:-1] + (1,), NEG_BIG, F32)], axis=-1)
    r = r.reshape(nh, 3, NA_ROWS_PER_BLOCK, NA_WIN_ROWS // 2, 2 * nd)
    full = jnp.einsum("htapk,kcz->htpacz", r, jnp.asarray(onehot), precision=lax.Precision.HIGHEST)
    return full.reshape(nh, 3, NA_WIN_ROWS // 2, NA_ROWS_PER_BLOCK * GRID_W, 2 * GRID_W)


def _lane_fold(x, op):
    acc = x[:, :128]
    for j in range(1, x.shape[1] // 128):
        acc = op(acc, x[:, 128 * j:128 * (j + 1)])
    return acc


def _online_update(state, s, v):
    v1 = jnp.concatenate([v, jnp.ones_like(v)], axis=1)
    mc = jnp.max(_lane_fold(s, jnp.maximum), axis=-1, keepdims=True)
    if state is None:
        return mc, _dot(jnp.exp2(s - mc).astype(BF16), v1)
    m, acc = state
    m_new = jnp.maximum(m, mc)
    p = jnp.exp2(s - m_new)
    return m_new, acc * jnp.exp2(m - m_new) + _dot(p.astype(BF16), v1)


def _softmax_finish(state):
    _, acc = state
    return acc[:, :128] / acc[:, 128:]


def _na_kernel(q_ref, k_ref, v_ref, kc_ref, vc_ref, bias_ref, o_ref, *, nblk):
    jb = pl.program_id(1)
    start = pl.multiple_of(jnp.clip(jb - 1, 0, nblk - 3) * 256, 256)
    nwin = NA_WIN_ROWS * GRID_W
    def scores(h):
        cols = slice(128 * h, 128 * (h + 1))
        q = q_ref[0, :, cols]
        bias = jnp.concatenate([bias_ref[h, 0, p] for p in range(bias_ref.shape[2])], axis=1)
        return _dot_nt(q, k_ref[0, pl.ds(start, nwin), cols]) + bias, _dot_nt(q, kc_ref[0, :, cols])

    s_next = scores(0)
    for h in range(N_HEADS):
        (s_w, s_c), s_next = s_next, (scores(h + 1) if h + 1 < N_HEADS else None)
        cols = slice(128 * h, 128 * (h + 1))
        state = _online_update(None, s_w, v_ref[0, pl.ds(start, nwin), cols])
        state = _online_update(state, s_c, vc_ref[0, :, cols])
        o_ref[0, :, cols] = _softmax_finish(state).astype(BF16)


def _na_attention(pa, pa_ctx, bias):
    b, s, _ = pa.shape
    c = pa_ctx.shape[1]
    rows = s // GRID_W
    nblk = rows // NA_ROWS_PER_BLOCK
    tq = NA_ROWS_PER_BLOCK * GRID_W
    nwin = NA_WIN_ROWS * GRID_W
    hw = N_HEADS * 128

    def bias_map(i, j):
        ty = (j > 0).astype(jnp.int32) + (j == nblk - 1).astype(jnp.int32)
        return (0, ty, 0, 0, 0)

    return pl.pallas_call(
        functools.partial(_na_kernel, nblk=nblk),
        grid=(b, nblk),
        in_specs=[
            pl.BlockSpec((1, tq, hw), lambda i, j: (i, j, 0)),
            pl.BlockSpec((1, s, hw), lambda i, j: (i, 0, 1)),
            pl.BlockSpec((1, s, hw), lambda i, j: (i, 0, 2)),
            pl.BlockSpec((1, c, hw), lambda i, j: (i, 0, 1)),
            pl.BlockSpec((1, c, hw), lambda i, j: (i, 0, 2)),
            pl.BlockSpec((N_HEADS, 1, nwin // 128, tq, 128), bias_map),
        ],
        out_specs=pl.BlockSpec((1, tq, hw), lambda i, j: (i, j, 0)),
        out_shape=jax.ShapeDtypeStruct((b, s, hw), BF16),
        compiler_params=_params("parallel", "arbitrary"),
        name="na_attention",
    )(pa, pa, pa, pa_ctx, pa_ctx, bias)


def _key_chunks(krefs, vrefs):
    chunks = []
    for k_ref, v_ref in zip(krefs, vrefs):
        n = k_ref.shape[1]
        chunks += [(k_ref, v_ref, st, min(BK, n - st)) for st in range(0, n, BK)]
    return chunks


def _head_cols(ref, h, width):
    return slice(0, width) if ref.shape[2] == width else slice(h * width, (h + 1) * width)


def _chained_softmax(chunks, scores, values):
    states = None
    for c in range(len(chunks)):
        s_cur = scores(c)
        states = states or [None] * len(s_cur)
        for i, s in enumerate(s_cur):
            states[i] = _online_update(states[i], s, values(c, i))
    return states


def _attn_kernel(q_ref, *refs, dk):
    o_ref = refs[-1]
    chunks = _key_chunks(refs[0:-1:2], refs[1:-1:2])

    def scores(c):
        k_ref, _, st, sz = chunks[c]
        return [_dot_nt(q_ref[0, :, h * dk:(h + 1) * dk], k_ref[0, st:st + sz, _head_cols(k_ref, h, dk)])
                for h in range(HEADS_PER_STEP)]

    def values(c, h):
        _, v_ref, st, sz = chunks[c]
        return v_ref[0, st:st + sz, _head_cols(v_ref, h, 128)]

    states = _chained_softmax(chunks, scores, values)
    for h in range(HEADS_PER_STEP):
        o_ref[0, :, 128 * h:128 * (h + 1)] = _softmax_finish(states[h]).astype(BF16)


def _diff_kernel(q_ref, *refs, lam_init):
    lq1_ref, lk1_ref, lq2_ref, lk2_ref, g_ref, o_ref = refs[-6:]
    chunks = _key_chunks(refs[0:-6:2], refs[1:-6:2])
    lam = (jnp.exp(jnp.sum(lq1_ref[...] * lk1_ref[...], axis=-1, keepdims=True))
           - jnp.exp(jnp.sum(lq2_ref[...] * lk2_ref[...], axis=-1, keepdims=True)) + lam_init)
    lane = lax.broadcasted_iota(jnp.int32, (1, 128), 1)
    low = (lane < DIFF_D).astype(BF16)
    qs = []
    for h in range(HEADS_PER_STEP):
        q = q_ref[0, :, 128 * h:128 * (h + 1)]
        qs += [q * low, q * (1.0 - low).astype(BF16)]

    def scores(c):
        k_ref, _, st, sz = chunks[c]
        return [_dot_nt(q, k_ref[0, st:st + sz, 128 * (i // 2):128 * (i // 2 + 1)]) for i, q in enumerate(qs)]

    def values(c, i):
        _, v_ref, st, sz = chunks[c]
        return v_ref[0, st:st + sz, 128 * (i // 2):128 * (i // 2 + 1)]

    states = _chained_softmax(chunks, scores, values)
    for h in range(HEADS_PER_STEP):
        o = _softmax_finish(states[2 * h]) - lam * _softmax_finish(states[2 * h + 1])
        o_ref[0, :, 128 * h:128 * (h + 1)] = (_rms(o, g_ref[...]) * (1.0 - lam_init)).astype(BF16)


def _attention(name, pq, p_lat, p_ctx, *, dk, k_off, v_off, shared_kv=False, diff=None):
    b, t, _ = pq.shape
    tq = min(TQ, t)
    hp = HEADS_PER_STEP
    kw, vw = (dk, 128) if shared_kv else (hp * dk, hp * 128)
    in_specs = [pl.BlockSpec((1, tq, hp * dk), lambda i, g, j: (i, j, g))]
    operands = [pq]
    for p in (p_lat, p_ctx):
        if p is None:
            continue
        n = p.shape[1]
        in_specs.append(pl.BlockSpec((1, n, kw), lambda i, g, j: (i, 0, k_off // kw + g)))
        in_specs.append(pl.BlockSpec((1, n, vw), lambda i, g, j: (i, 0, v_off // vw + g)))
        operands += [p, p]
    if diff is None:
        kernel = functools.partial(_attn_kernel, dk=dk)
    else:
        lam_vecs, g_sub, lam_init = diff
        kernel = functools.partial(_diff_kernel, lam_init=lam_init)
        in_specs += [_const_spec((1, DIFF_D))] * 4 + [_const_spec((1, 128))]
        operands += [v.reshape(1, DIFF_D) for v in lam_vecs] + [g_sub.reshape(1, 128)]
    return pl.pallas_call(
        kernel,
        grid=(b, N_HEADS // hp, t // tq),
        in_specs=in_specs,
        out_specs=pl.BlockSpec((1, tq, hp * 128), lambda i, g, j: (i, j, g)),
        out_shape=jax.ShapeDtypeStruct((b, t, N_HEADS * 128), BF16),
        compiler_params=_params("parallel", "parallel", "arbitrary"),
        name=name,
    )(*operands)


def _out_kernel(oa_ref, ob_ref, oc_ref, od_ref, w_ref, x_ref, gt_ref, g_ref, sc_ref, sh_ref,
                xo_ref, ho_ref, *, ctx_row):
    gate, sc, sh = _mod_row(gt_ref, ctx_row), _mod_row(sc_ref, ctx_row), _mod_row(sh_ref, ctx_row)
    for rows in _row_subtiles(x_ref.shape[1]):
        y = _dot(oa_ref[0, rows, :], w_ref[0, 0:512, :])
        y = y + _dot(ob_ref[0, rows, :], w_ref[0, 512:1024, :])
        y = y + _dot(oc_ref[0, rows, :], w_ref[0, 1024:1536, :])
        y = y + _dot(od_ref[0, rows, :], w_ref[0, 1536:2048, :])
        xn = x_ref[0, rows, :] + gate * y
        xo_ref[0, rows, :] = xn
        ho_ref[0, rows, :] = _modnorm(xn, g_ref[...], sc, sh).astype(BF16)


def _out_proj(outs, w_out, x, g_ffn, mod, layer, ctx_row):
    b, t, d = x.shape
    tm = min(TM_DENSE, t)
    o_spec = pl.BlockSpec((1, tm, 512), lambda i, j: (i, j, 0))
    x_spec = pl.BlockSpec((1, tm, d), lambda i, j: (i, j, 0))
    return pl.pallas_call(
        functools.partial(_out_kernel, ctx_row=ctx_row),
        grid=(b, t // tm),
        in_specs=[o_spec] * 4 + [
            pl.BlockSpec((1,) + w_out.shape[1:], lambda i, j: (layer, 0, 0)),
            x_spec,
            _mod_spec(layer, 2),
            _const_spec((1, d)),
            _mod_spec(layer, 4),
            _mod_spec(layer, 3),
        ],
        out_specs=[x_spec, x_spec],
        out_shape=[jax.ShapeDtypeStruct((b, t, d), F32), jax.ShapeDtypeStruct((b, t, d), BF16)],
        compiler_params=_params("parallel", "parallel"),
        name="out_proj",
    )(*outs, w_out, x, mod, g_ffn.reshape(1, d), mod, mod)


def _ffn_up_kernel(h_ref, wg_ref, wu_ref, *rest, cast_wd):
    if cast_wd:
        wdi_ref, o_ref, wdo_ref, wgb_ref, wub_ref = rest
        wdo_ref[...] = wdi_ref[0].astype(BF16)
    else:
        o_ref, wgb_ref, wub_ref = rest

    @pl.when((pl.program_id(1) == 0) & (pl.program_id(2) == 0))
    def _():
        wgb_ref[...] = wg_ref[0].astype(BF16)
        wub_ref[...] = wu_ref[0].astype(BF16)

    for rows in _row_subtiles(h_ref.shape[1]):
        h = h_ref[0, rows, :]
        gate = _dot(h, wgb_ref[...])
        up = _dot(h, wub_ref[...])
        o_ref[0, rows, :] = ((gate / (1.0 + jnp.exp(-gate))) * up).astype(BF16)


def _ffn_up(h, wg, wu, layer, wd=None):
    b, t, d = h.shape
    dff = wg.shape[2]
    tm = min(TM_UP, t)
    nb, nt = b, t // tm
    w_spec = pl.BlockSpec((1, d, TF), lambda f, i, j: (layer, 0, f))
    in_specs = [pl.BlockSpec((1, tm, d), lambda f, i, j: (i, j, 0)), w_spec, w_spec]
    out_specs = [pl.BlockSpec((1, tm, TF), lambda f, i, j: (i, j, f))]
    out_shape = [jax.ShapeDtypeStruct((b, t, dff), BF16)]
    operands = [h, wg, wu]
    if wd is not None:
        slab = dff // (dff // TF * nb * nt)
        in_specs.append(pl.BlockSpec((1, slab, d), lambda f, i, j: (layer, (f * nb + i) * nt + j, 0)))
        out_specs.append(pl.BlockSpec((slab, d), lambda f, i, j: ((f * nb + i) * nt + j, 0)))
        out_shape.append(jax.ShapeDtypeStruct((dff, d), BF16))
        operands.append(wd)
    outs = pl.pallas_call(
        functools.partial(_ffn_up_kernel, cast_wd=wd is not None),
        grid=(dff // TF, nb, nt),
        in_specs=in_specs,
        out_specs=out_specs,
        out_shape=out_shape,
        scratch_shapes=[pltpu.VMEM((d, TF), BF16), pltpu.VMEM((d, TF), BF16)],
        compiler_params=_params("arbitrary", "arbitrary", "arbitrary"),
        name="ffn_up",
    )(*operands)
    return outs if wd is not None else outs[0]


def _ffn_down_kernel(a_ref, wd_ref, x_ref, gt_ref, g_ref, *rest, ctx_row, final):
    xn = x_ref[0] + _mod_row(gt_ref, ctx_row) * _dot(a_ref[0], wd_ref[...])
    if final:
        (o_ref,) = rest
        o_ref[0] = _rms(xn, g_ref[...])
    else:
        sc_ref, sh_ref, xo_ref, ho_ref = rest
        xo_ref[0] = xn
        ho_ref[0] = _modnorm(xn, g_ref[...], _mod_row(sc_ref, ctx_row),
                             _mod_row(sh_ref, ctx_row)).astype(BF16)


def _ffn_down(act, wd, x, g_next, mod, layer, ctx_row, final):
    b, t, d = x.shape
    dff = wd.shape[0]
    tm = min(TM_DOWN, t)
    row_spec = pl.BlockSpec((1, tm, d), lambda i, j: (i, j, 0))
    in_specs = [
        pl.BlockSpec((1, tm, dff), lambda i, j: (i, j, 0)),
        pl.BlockSpec((dff, d), lambda i, j: (0, 0), pipeline_mode=pl.Buffered(1)),
        row_spec,
        _mod_spec(layer, 5),
        _const_spec((1, d)),
    ]
    operands = [act, wd, x, mod, g_next.reshape(1, d)]
    if final:
        out_specs = row_spec
        out_shape = jax.ShapeDtypeStruct((b, t, d), F32)
    else:
        in_specs += [_mod_spec(layer + 1, 1), _mod_spec(layer + 1, 0)]
        operands += [mod, mod]
        out_specs = [row_spec, row_spec]
        out_shape = [jax.ShapeDtypeStruct((b, t, d), F32), jax.ShapeDtypeStruct((b, t, d), BF16)]
    return pl.pallas_call(
        functools.partial(_ffn_down_kernel, ctx_row=ctx_row, final=final),
        grid=(b, t // tm),
        in_specs=in_specs,
        out_specs=out_specs,
        out_shape=out_shape,
        compiler_params=_params("parallel", "parallel"),
        name="ffn_down",
    )(*operands)


def _ffn(h, wg, wu, wd, x, g_next, mod, layer, ctx_row, final):
    if wd.ndim == 3:
        act, wd = _ffn_up(h, wg, wu, layer, wd)
    else:
        act = _ffn_up(h, wg, wu, layer)
    return _ffn_down(act, wd, x, g_next, mod, layer, ctx_row, final), wd


def _rope_tables(s, n_ctx):
    t = jnp.arange(s, dtype=jnp.int32)
    row = (t // GRID_W).astype(F32)
    col = (t % GRID_W).astype(F32)

    def cos_sin(dim):
        d_axis = dim // 2
        inv = 1.0 / (ROPE_THETA ** (jnp.arange(0, d_axis, 2, dtype=F32) / d_axis))
        ang = jnp.concatenate([row[:, None] * inv, col[:, None] * inv], axis=-1)
        return jnp.cos(ang), jnp.sin(ang)

    c128, s128 = cos_sin(HEAD_DIM)
    c64, s64 = cos_sin(MLA_ROPE)
    z = jnp.zeros_like(s64)
    lat = (
        jnp.concatenate([c128, c128], axis=-1),
        jnp.concatenate([-s128, s128], axis=-1),
        jnp.tile(c64, (1, 4)),
        jnp.tile(jnp.concatenate([-s64, z], axis=-1), (1, 2)),
        jnp.tile(jnp.concatenate([z, s64], axis=-1), (1, 2)),
    )
    one = jnp.ones((n_ctx, 128), F32)
    zero = jnp.zeros((n_ctx, 128), F32)
    return lat, (one, zero, one, zero, zero)


def kernel(x, c, ctx, c_ctx, w_ada, b_ada, g_attn, g_ffn, w_in, w_out, na_rpb, gqa_gq, gqa_gk,
           mla_gq, mla_gkv, mla_wuq, mla_wukv, diff_lq1, diff_lk1, diff_lq2, diff_lk2, diff_gsub,
           ffn_wg, ffn_wu, ffn_wd, g_final):
    depth = w_ada.shape[0]
    b, s, d = x.shape
    n_ctx = ctx.shape[1]
    rows = s // GRID_W
    ctx_row = b

    mod = _modulation(c, c_ctx, w_ada, b_ada)
    w_all, wo = _pack_weights(w_in, w_out)
    wg, wu = ffn_wg, ffn_wu
    tabs_lat, tabs_ctx = _rope_tables(s, b * n_ctx)

    def per_sample(a):
        return a.reshape(b, n_ctx, a.shape[-1])

    def flat(a):
        return a.reshape(1, b * n_ctx, a.shape[-1])

    xl, xc = x, flat(ctx)
    hl, hc = xl, xc

    for l in range(depth):
        ctx_out = l < depth - 1
        lam_init = 0.8 - 0.6 * math.exp(-0.3 * l)
        wuq = jnp.pad(mla_wuq[l].astype(BF16).reshape(MLA_Q_RANK, N_HEADS, MLA_NOPE + MLA_ROPE),
                      ((0, 0), (0, 0), (0, 256 - MLA_NOPE - MLA_ROPE))).reshape(MLA_Q_RANK, N_HEADS * 256)
        wukv = mla_wukv[l].astype(BF16).reshape(MLA_KV_RANK, N_HEADS, 2, 128)
        wukv = wukv.transpose(0, 2, 1, 3).reshape(MLA_KV_RANK, 2 * N_HEADS * 128)
        gq, gk = gqa_gq[l].reshape(1, 128), gqa_gk[l].reshape(1, 128)
        mgq, mgkv = mla_gq[l].reshape(1, MLA_Q_RANK), mla_gkv[l].reshape(1, MLA_KV_RANK)
        bias = _na_bias(na_rpb[l], rows)
        diff = ((diff_lq1[l], diff_lk1[l], diff_lq2[l], diff_lk2[l]), diff_gsub[l], lam_init)

        small = [gq, gk, mgq, mgkv, wuq, wukv]
        norm = (g_attn[0], mod) if l == 0 else None
        pa, pb, pc, pd = _in_proj(hl, w_all, l, small, tabs_lat, norm)
        pa_c, pb_c, pc_c, pd_c = (per_sample(p) for p in _in_proj(hc, w_all, l, small, tabs_ctx, norm, ctx_row))

        gqa_idx = dict(dk=128, k_off=512, v_off=768, shared_kv=True)
        mla_idx = dict(dk=256, k_off=1024, v_off=2048)
        std_idx = dict(dk=128, k_off=512, v_off=1024)

        outs = (
            _na_attention(pa, pa_c, bias),
            _attention("gqa_attention", pb, pb, pb_c, **gqa_idx),
            _attention("mla_attention", pc, pc, pc_c, **mla_idx),
            _attention("diff_attention", pd, pd, pd_c, diff=diff, **std_idx),
        )
        xl, hl = _out_proj(outs, wo, xl, g_ffn[l], mod, l, None)
        if ctx_out:
            outs_c = (
                _attention("na_ctx_attention", pa_c, None, pa_c, **std_idx),
                _attention("gqa_ctx_attention", pb_c, None, pb_c, **gqa_idx),
                _attention("mla_ctx_attention", pc_c, None, pc_c, **mla_idx),
                _attention("diff_ctx_attention", pd_c, None, pd_c, diff=diff, **std_idx),
            )
            xc, hc = _out_proj([flat(o) for o in outs_c], wo, xc, g_ffn[l], mod, l, ctx_row)
            (xl, hl), wd = _ffn(hl, wg, wu, ffn_wd, xl, g_attn[l + 1], mod, l, None, False)
            (xc, hc), _ = _ffn(hc, wg, wu, wd, xc, g_attn[l + 1], mod, l, ctx_row, False)
        else:
            xl, _ = _ffn(hl, wg, wu, ffn_wd, xl, g_final, mod, l, None, True)
    return xl
```
